```python
import jax, jax.numpy as jnp
from jax import lax
import numpy as np

D_MODEL = 2048
BATCH = 8
SEQ = 4096
DEPTH = 4

N_MIXERS = 4
RMS_EPS = 1e-6
HGRN_EXPAND = 128
HGRN_HEADS = D_MODEL // HGRN_EXPAND
HGRN_DK = HGRN_EXPAND
HGRN_DV = D_MODEL // HGRN_HEADS
HGRN_WIDTH = HGRN_HEADS * HGRN_DK
HGRN_CHUNK = 64
SWA_HEAD_DIM = 64
SWA_Q_HEADS = D_MODEL // SWA_HEAD_DIM
SWA_KV_HEADS = SWA_Q_HEADS // 8
SWA_WINDOW = 128
SCONV_WIDTH = 3
FOX_HEAD_DIM = 64
FOX_HEADS = D_MODEL // FOX_HEAD_DIM
FOX_BLOCK = 128
ROPE_THETA = 500000.0
ROT_DIM = SWA_HEAD_DIM // 4
D_FF = 5632
FFN_CONV_WIDTH = 3

kernel_name = "hybrid_interleaved_hgrn2_swa_sconv_fox"


def rmsnorm(x, g):
    xf = x.astype(jnp.float32)
    y = xf * lax.rsqrt(jnp.mean(xf * xf, axis=-1, keepdims=True) + RMS_EPS)
    return (y * g.astype(jnp.float32)).astype(x.dtype)


def causal_dwconv(x, w):
    K, C = w.shape
    return lax.conv_general_dilated(
        x, w[:, None, :].astype(x.dtype), window_strides=(1,), padding=[(K - 1, 0)],
        dimension_numbers=("NWC", "WIO", "NWC"), feature_group_count=C)


def partial_rope(x, positions):
    half = ROT_DIM // 2
    inv_freq = ROPE_THETA ** (-jnp.arange(half, dtype=jnp.float32) / half)
    ang = positions.astype(jnp.float32)[:, None] * inv_freq[None, :]
    cos = jnp.cos(ang)[None, :, None, :]
    sin = jnp.sin(ang)[None, :, None, :]
    xf = x.astype(jnp.float32)
    x1, x2 = xf[..., :half], xf[..., half:ROT_DIM]
    out = jnp.concatenate([x1 * cos - x2 * sin, x2 * cos + x1 * sin, xf[..., ROT_DIM:]], axis=-1)
    return out.astype(x.dtype)


def hgrn2_mixer(h, w_in, w_out, norm_g, lb):
    Bsz, T, _ = h.shape
    C = HGRN_CHUNK
    nC = T // C
    q, f, i, g = jnp.split(h @ w_in, 4, axis=-1)
    q = jax.nn.silu(q.astype(jnp.float32))
    f = lb + (1.0 - lb) * jax.nn.sigmoid(f.astype(jnp.float32))
    log_f = jnp.log(f)
    k = 1.0 - f
    v = i.astype(jnp.float32)

    def to_chunks(a, d):
        return a.reshape(Bsz, nC, C, HGRN_HEADS, d).transpose(1, 0, 3, 2, 4)

    qc, kc, vc = to_chunks(q, HGRN_DK), to_chunks(k, HGRN_DK), to_chunks(v, HGRN_DV)
    bc = jnp.cumsum(to_chunks(log_f, HGRN_DK), axis=3)
    causal = jnp.tril(jnp.ones((C, C), dtype=bool))

    def chunk_step(S, inp):
        qb, kb, vb, bb = inp
        inter = jnp.einsum("bhtk,bhkv->bhtv", qb * jnp.exp(bb), S)
        rel = jnp.where(causal[:, :, None], bb[:, :, :, None, :] - bb[:, :, None, :, :], -jnp.inf)
        A = jnp.einsum("bhtk,bhsk,bhtsk->bhts", qb, kb, jnp.exp(rel))
        intra = jnp.einsum("bhts,bhsv->bhtv", A, vb)
        b_last = bb[:, :, -1:, :]
        S_new = jnp.exp(b_last[:, :, 0, :])[..., None] * S + jnp.einsum(
            "bhsk,bhsv->bhkv", kb * jnp.exp(b_last - bb), vb)
        return S_new, inter + intra

    S0 = jnp.zeros((Bsz, HGRN_HEADS, HGRN_DK, HGRN_DV), jnp.float32)
    _, o = lax.scan(chunk_step, S0, (qc, kc, vc, bc))
    o = o.transpose(1, 0, 3, 2, 4).reshape(Bsz, T, HGRN_HEADS, HGRN_DV)
    o = rmsnorm(o, norm_g).reshape(Bsz, T, HGRN_HEADS * HGRN_DV)
    o = o * jax.nn.silu(g.astype(jnp.float32))
    return o.astype(h.dtype) @ w_out


def swa_sink_mixer(h, positions, w_in, w_out, sinks):
    Bsz, T, _ = h.shape
    W, d = SWA_WINDOW, SWA_HEAD_DIM
    KV, G = SWA_KV_HEADS, SWA_Q_HEADS // SWA_KV_HEADS
    nblk = T // W
    q, k, v = jnp.split(h @ w_in, [SWA_Q_HEADS * d, SWA_Q_HEADS * d + KV * d], axis=-1)
    q = partial_rope(q.reshape(Bsz, T, SWA_Q_HEADS, d), positions)
    k = partial_rope(k.reshape(Bsz, T, KV, d), positions)
    v = v.reshape(Bsz, T, KV, d)
    qb = q.reshape(Bsz, nblk, W, KV, G, d)

    def band(a):
        cur = a.reshape(Bsz, nblk, W, KV, d)
        prev = jnp.concatenate([jnp.zeros_like(cur[:, :1]), cur[:, :-1]], axis=1)
        return jnp.concatenate([prev, cur], axis=2)

    kb, vb = band(k), band(v)
    s = jnp.einsum("bnqhgd,bnkhd->bnhgqk", qb, kb).astype(jnp.float32) * (d ** -0.5)
    qi = jnp.arange(W)[:, None]
    kj = jnp.arange(2 * W)[None, :]
    diff = qi + W - kj
    blk = jnp.arange(nblk)[:, None, None]
    allowed = (diff >= 0) & (diff < W) & (blk * W + kj - W >= 0)
    s = jnp.where(allowed[None, :, None, None], s, -jnp.inf)
    sink = sinks.astype(jnp.float32).reshape(KV, G)[None, None, :, :, None, None]
    m = jnp.maximum(jnp.max(s, axis=-1, keepdims=True), sink)
    e = jnp.exp(s - m)
    p = e / (jnp.sum(e, axis=-1, keepdims=True) + jnp.exp(sink - m))
    o = jnp.einsum("bnhgqk,bnkhd->bnqhgd", p.astype(h.dtype), vb)
    return o.reshape(Bsz, T, SWA_Q_HEADS * d) @ w_out


def short_conv_mixer(h, w_in, conv_w, w_out):
    b_gate, c_gate, xv = jnp.split(h @ w_in, 3, axis=-1)
    return (b_gate * causal_dwconv(c_gate * xv, conv_w)) @ w_out


def fox_mixer(h, w_in, b_f, w_out):
    Bsz, T, _ = h.shape
    H, d, W = FOX_HEADS, FOX_HEAD_DIM, FOX_BLOCK
    width = H * d
    nblk = T // W
    q, k, v, f_logit, g = jnp.split(h @ w_in, [width, 2 * width, 3 * width, 3 * width + H], axis=-1)
    q = q.reshape(Bsz, T, H, d)
    k = k.reshape(Bsz, T, H, d)
    v = v.reshape(Bsz, T, H, d)
    log_f = jax.nn.log_sigmoid(f_logit.astype(jnp.float32) + b_f.astype(jnp.float32))
    c = jnp.cumsum(log_f, axis=1).transpose(0, 2, 1)
    key_pos = jnp.arange(T)

    def q_block(n):
        start = n * W
        qs = lax.dynamic_slice_in_dim(q, start, W, axis=1)
        cq = lax.dynamic_slice_in_dim(c, start, W, axis=2)
        s = jnp.einsum("bqhd,bkhd->bhqk", qs, k).astype(jnp.float32) * (d ** -0.5)
        s = s + cq[..., None] - c[:, :, None, :]
        q_pos = start + jnp.arange(W)
        s = jnp.where((key_pos[None, :] <= q_pos[:, None])[None, None], s, -jnp.inf)
        p = jax.nn.softmax(s, axis=-1)
        return jnp.einsum("bhqk,bkhd->bqhd", p.astype(v.dtype), v)

    o = lax.map(q_block, jnp.arange(nblk))
    o = o.transpose(1, 0, 2, 3, 4).reshape(Bsz, T, width)
    o = o * jax.nn.sigmoid(g.astype(jnp.float32)).astype(o.dtype)
    return o @ w_out


def conv_glu_ffn(h, w_up, conv_w, conv_b, w_down):
    u = causal_dwconv(h @ w_up, conv_w) + conv_b.astype(h.dtype)
    gate, up = jnp.split(u, 2, axis=-1)
    return (jax.nn.silu(gate) * up) @ w_down


def setup_inputs(seed: int = 0) -> dict:
    key = jax.random.key(seed)
    ks = iter(jax.random.split(key, 32))

    def nrm(shape, scale):
        return scale * jax.random.normal(next(ks), shape, jnp.float32)

    n_of = [len(range(m, DEPTH, N_MIXERS)) for m in range(N_MIXERS)]
    nA, nB, nC, nD = n_of
    D = D_MODEL
    sd = D ** -0.5
    return {
        "x": nrm((BATCH, SEQ, D), 1.0),
        "positions": jnp.arange(SEQ, dtype=jnp.int32),
        "mix_pre_g": 1.0 + nrm((DEPTH, D), 0.05),
        "mix_post_g": 1.0 + nrm((DEPTH, D), 0.05),
        "ffn_pre_g": 1.0 + nrm((DEPTH, D), 0.05),
        "ffn_post_g": 1.0 + nrm((DEPTH, D), 0.05),
        "hgrn_w_in": nrm((nA, D, 3 * HGRN_WIDTH + HGRN_HEADS * HGRN_DV), sd),
        "hgrn_w_out": nrm((nA, HGRN_HEADS * HGRN_DV, D), (HGRN_HEADS * HGRN_DV) ** -0.5),
        "hgrn_norm_g": 1.0 + nrm((nA, HGRN_DV), 0.05),
        "hgrn_lb_param": nrm((DEPTH + 1, HGRN_WIDTH), 0.5),
        "swa_w_in": nrm((nB, D, (SWA_Q_HEADS + 2 * SWA_KV_HEADS) * SWA_HEAD_DIM), sd),
        "swa_w_out": nrm((nB, SWA_Q_HEADS * SWA_HEAD_DIM, D), (SWA_Q_HEADS * SWA_HEAD_DIM) ** -0.5),
        "swa_sinks": nrm((nB, SWA_Q_HEADS), 0.5),
        "sc_w_in": nrm((nC, D, 3 * D), sd),
        "sc_conv_w": nrm((nC, SCONV_WIDTH, D), SCONV_WIDTH ** -0.5),
        "sc_w_out": nrm((nC, D, D), sd),
        "fox_w_in": nrm((nD, D, 4 * FOX_HEADS * FOX_HEAD_DIM + FOX_HEADS), sd),
        "fox_b_f": 3.0 + nrm((nD, FOX_HEADS), 0.5),
        "fox_w_out": nrm((nD, FOX_HEADS * FOX_HEAD_DIM, D), (FOX_HEADS * FOX_HEAD_DIM) ** -0.5),
        "ffn_w_up": nrm((DEPTH, D, 2 * D_FF), sd),
        "ffn_conv_w": nrm((DEPTH, FFN_CONV_WIDTH, 2 * D_FF), FFN_CONV_WIDTH ** -0.5),
        "ffn_conv_b": nrm((DEPTH, 2 * D_FF), 0.02),
        "ffn_w_down": nrm((DEPTH, D_FF, D), D_FF ** -0.5),
    }


def reference(x, positions, mix_pre_g, mix_post_g, ffn_pre_g, ffn_post_g,
              hgrn_w_in, hgrn_w_out, hgrn_norm_g, hgrn_lb_param,
              swa_w_in, swa_w_out, swa_sinks,
              sc_w_in, sc_conv_w, sc_w_out,
              fox_w_in, fox_b_f, fox_w_out,
              ffn_w_up, ffn_conv_w, ffn_conv_b, ffn_w_down):
    lb_table = jnp.cumsum(jax.nn.softmax(hgrn_lb_param.astype(jnp.float32), axis=0), axis=0)
    for i in range(DEPTH):
        m, j = i % N_MIXERS, i // N_MIXERS
        hn = rmsnorm(x, mix_pre_g[i])
        if m == 0:
            y = hgrn2_mixer(hn, hgrn_w_in[j], hgrn_w_out[j], hgrn_norm_g[j], lb_table[i])
        elif m == 1:
            y = swa_sink_mixer(hn, positions, swa_w_in[j], swa_w_out[j], swa_sinks[j])
        elif m == 2:
            y = short_conv_mixer(hn, sc_w_in[j], sc_conv_w[j], sc_w_out[j])
        else:
            y = fox_mixer(hn, fox_w_in[j], fox_b_f[j], fox_w_out[j])
        x = x + rmsnorm(y.astype(x.dtype), mix_post_g[i])
        hn = rmsnorm(x, ffn_pre_g[i])
        y = conv_glu_ffn(hn, ffn_w_up[i], ffn_conv_w[i], ffn_conv_b[i], ffn_w_down[i])
        x = x + rmsnorm(y.astype(x.dtype), ffn_post_g[i])
    return x
```

```python
import functools
import math

import numpy as np
import jax
import jax.numpy as jnp
from jax import lax
from jax.experimental import pallas as pl
from jax.experimental.pallas import tpu as pltpu

F32 = jnp.float32
BF16 = jnp.bfloat16

RMS_EPS = 1e-6
N_MIXERS = 4
HGRN_HEAD_DIM = 128
HGRN_CHUNK = 64
SWA_HEAD_DIM = 64
SWA_GROUP = 8
SWA_WINDOW = 128
ROPE_THETA = 500000.0
ROT_DIM = 16
FOX_HEAD_DIM = 64
FFN_CONV_WIDTH = 3

LANES = 128
SUBLANES = 8
V7X_VMEM_BYTES = 64 * 2**20
VMEM_CAP_BYTES = 56 * 2**20


def _vmem_limit(*nbytes):
    return int(min(VMEM_CAP_BYTES, sum(nbytes)))


def _nbytes(shape, dtype):
    return int(np.prod(shape)) * jnp.dtype(dtype).itemsize


def _rms(x, g):
    ms = jnp.mean(x * x, axis=-1, keepdims=True)
    return x * lax.rsqrt(ms + RMS_EPS) * g


def _dot(a, b):
    return jnp.dot(a, b, preferred_element_type=F32)


def _dot_nt(a, b):
    return lax.dot_general(a, b, (((1,), (1,)), ((), ())), preferred_element_type=F32)


def _dot_tn(a, b):
    return lax.dot_general(a, b, (((0,), (0,)), ((), ())), preferred_element_type=F32)


def _dot_01_f32(m01, x):
    n = x.shape[1]
    hi = x.astype(BF16)
    r1 = x - hi.astype(F32)
    mid = r1.astype(BF16)
    lo = (r1 - mid.astype(F32)).astype(BF16)
    y = _dot(m01, jnp.concatenate([hi, mid, lo], axis=1))
    return (y[:, :n] + y[:, n:2 * n]) + y[:, 2 * n:]


def _sigmoid(x):
    return 1.0 / (1.0 + jnp.exp(-x))


def _silu(x):
    return x * _sigmoid(x)


def _causal_conv3(u, prev8, w):
    w0, w1, w2 = w[0:1], w[1:2], w[2:3]
    r1 = pltpu.roll(u, 1, 0)
    r2 = pltpu.roll(u, 2, 0)
    y = w2 * u + w1 * r1 + w0 * r2
    rows = lax.broadcasted_iota(jnp.int32, prev8.shape, 0)
    h1 = jnp.where(rows < 1, pltpu.roll(prev8, 1, 0), r1[:SUBLANES])
    h2 = jnp.where(rows < 2, pltpu.roll(prev8, 2, 0), r2[:SUBLANES])
    yh = w2 * u[:SUBLANES] + w1 * h1 + w0 * h2
    return jnp.concatenate([yh, y[SUBLANES:]], axis=0)


def _norm_matmul_kernel(x_ref, g_ref, w_ref, o_ref, hn_ref):
    @pl.when(pl.program_id(1) == 0)
    def _():
        hn_ref[...] = _rms(x_ref[...], g_ref[...]).astype(BF16)

    o_ref[...] = _dot(hn_ref[...], w_ref[...]).astype(o_ref.dtype)


def _norm_matmul(x, g, w, *, tm, tn, out_dtype=F32):
    M, D = x.shape
    N = w.shape[1]
    assert M % tm == 0 and N % tn == 0
    limit = _vmem_limit(2 * _nbytes((tm, D), F32), _nbytes((tm, D), BF16), 2 * _nbytes((D, tn), BF16),
                        3 * _nbytes((tm, tn), F32), _nbytes((tm, D), F32), 4 * 2**20)
    return pl.pallas_call(
        _norm_matmul_kernel,
        grid=(M // tm, N // tn),
        in_specs=[pl.BlockSpec((tm, D), lambda i, j: (i, 0)),
                  pl.BlockSpec((1, D), lambda i, j: (0, 0)),
                  pl.BlockSpec((D, tn), lambda i, j: (0, j))],
        out_specs=pl.BlockSpec((tm, tn), lambda i, j: (i, j)),
        out_shape=jax.ShapeDtypeStruct((M, N), out_dtype),
        scratch_shapes=[pltpu.VMEM((tm, D), BF16)],
        compiler_params=pltpu.CompilerParams(dimension_semantics=("parallel", "arbitrary"),
                                             vmem_limit_bytes=limit),
        name="norm_matmul",
    )(x, g.reshape(1, D), w)


def _out_proj_kernel(a_ref, w_ref, g_ref, x_ref, o_ref):
    y = _dot(a_ref[...], w_ref[...])
    o_ref[...] = x_ref[...] + _rms(y, g_ref[...])


def _out_proj(a, w, g, x, *, tm):
    M, K = a.shape
    D = w.shape[1]
    assert M % tm == 0
    limit = _vmem_limit(2 * _nbytes((tm, K), BF16), 2 * _nbytes((K, D), BF16), 4 * _nbytes((tm, D), F32),
                        2 * _nbytes((tm, D), F32), 4 * 2**20)
    return pl.pallas_call(
        _out_proj_kernel,
        grid=(M // tm,),
        in_specs=[pl.BlockSpec((tm, K), lambda i: (i, 0)),
                  pl.BlockSpec((K, D), lambda i: (0, 0)),
                  pl.BlockSpec((1, D), lambda i: (0, 0)),
                  pl.BlockSpec((tm, D), lambda i: (i, 0))],
        out_specs=pl.BlockSpec((tm, D), lambda i: (i, 0)),
        out_shape=jax.ShapeDtypeStruct((M, D), F32),
        compiler_params=pltpu.CompilerParams(dimension_semantics=("parallel",), vmem_limit_bytes=limit),
        name="out_proj",
    )(a, w, g.reshape(1, D), x)


def _ffn_kernel(x_ref, gpre_ref, wg_ref, wu_ref, cwg_ref, cwu_ref, cbg_ref, cbu_ref, wd_ref, gpost_ref,
                o_ref, hn_ref, acc_ref, carry_ref):
    ti = pl.program_id(1)
    c = pl.program_id(2)
    nc = pl.num_programs(2)
    tm = x_ref.shape[0]

    @pl.when(c == 0)
    def _():
        hn_ref[...] = _rms(x_ref[...], gpre_ref[...]).astype(BF16)

    @pl.when(ti == 0)
    def _():
        carry_ref[c] = jnp.zeros(carry_ref.shape[1:], F32)

    hn = hn_ref[...]
    ug = _dot(hn, wg_ref[...])
    uu = _dot(hn, wu_ref[...])
    yg = _causal_conv3(ug, carry_ref[c, 0], cwg_ref[...]) + cbg_ref[...]
    yu = _causal_conv3(uu, carry_ref[c, 1], cwu_ref[...]) + cbu_ref[...]
    carry_ref[c, 0] = ug[tm - SUBLANES:]
    carry_ref[c, 1] = uu[tm - SUBLANES:]
    act = (_silu(yg) * yu).astype(BF16)
    part = _dot(act, wd_ref[...])

    @pl.when(c == 0)
    def _():
        acc_ref[...] = part

    @pl.when(c > 0)
    def _():
        acc_ref[...] += part

    @pl.when(c == nc - 1)
    def _():
        o_ref[...] = x_ref[...] + _rms(acc_ref[...], gpost_ref[...])


def _ffn(x, B, T, g_pre, w_up, conv_w, conv_b, w_down, g_post, *, tm, tf):
    M, D = x.shape
    F = w_down.shape[0]
    assert T % tm == 0 and F % tf == 0 and tm % SUBLANES == 0
    nt, nc = T // tm, F // tf
    conv_b = conv_b.reshape(1, 2 * F)
    limit = _vmem_limit(4 * _nbytes((tm, D), F32), _nbytes((tm, D), BF16), _nbytes((tm, D), F32),
                        4 * _nbytes((D, tf), BF16), 2 * _nbytes((tf, D), BF16),
                        8 * _nbytes((tm, tf), F32), _nbytes((tm, D), F32), 4 * 2**20)
    return pl.pallas_call(
        _ffn_kernel,
        grid=(B, nt, nc),
        in_specs=[pl.BlockSpec((tm, D), lambda b, t, c: (b * nt + t, 0)),
                  pl.BlockSpec((1, D), lambda b, t, c: (0, 0)),
                  pl.BlockSpec((D, tf), lambda b, t, c: (0, c)),
                  pl.BlockSpec((D, tf), lambda b, t, c: (0, nc + c)),
                  pl.BlockSpec((FFN_CONV_WIDTH, tf), lambda b, t, c: (0, c)),
                  pl.BlockSpec((FFN_CONV_WIDTH, tf), lambda b, t, c: (0, nc + c)),
                  pl.BlockSpec((1, tf), lambda b, t, c: (0, c)),
                  pl.BlockSpec((1, tf), lambda b, t, c: (0, nc + c)),
                  pl.BlockSpec((tf, D), lambda b, t, c: (c, 0)),
                  pl.BlockSpec((1, D), lambda b, t, c: (0, 0))],
        out_specs=pl.BlockSpec((tm, D), lambda b, t, c: (b * nt + t, 0)),
        out_shape=jax.ShapeDtypeStruct((M, D), F32),
        scratch_shapes=[pltpu.VMEM((tm, D), BF16),
                        pltpu.VMEM((tm, D), F32),
                        pltpu.VMEM((nc, 2, SUBLANES, tf), F32)],
        compiler_params=pltpu.CompilerParams(dimension_semantics=("parallel", "arbitrary", "arbitrary"),
                                             vmem_limit_bytes=limit),
        name="ffn",
    )(x, g_pre.reshape(1, D), w_up, w_up, conv_w, conv_w, conv_b, conv_b, w_down, g_post.reshape(1, D))


def _hgrn_sum_matrix(C):
    L = int(math.log2(C))
    t = np.arange(C)[:, None]
    j = np.arange(C)[None, :]
    blocks = [j <= t, j > t]
    for l in range(L):
        half, blk = 1 << l, 2 << l
        mid = t - t % blk + half - 1
        is_query = (t % blk) >= half
        blocks.append(np.where(is_query, (j > mid) & (j <= t), (j > t) & (j <= mid)))
    return np.concatenate(blocks, axis=0).astype(np.float32)


def _hgrn_kernel(q_ref, f_ref, i_ref, g_ref, lbp_ref, ng_ref, sm_ref, o_ref, st_ref, *, layer, C):
    tc = q_ref.shape[0]
    L = int(math.log2(C))

    @pl.when(pl.program_id(2) == 0)
    def _():
        st_ref[...] = jnp.zeros(st_ref.shape, F32)

    p = lbp_ref[...]
    e = jnp.exp(p - jnp.max(p, axis=0, keepdims=True))
    lb = jnp.sum(e[:layer + 1], axis=0, keepdims=True) / jnp.sum(e, axis=0, keepdims=True)
    ng = ng_ref[...]
    sm = sm_ref[...]

    row = lax.broadcasted_iota(jnp.int32, (C, C), 0)
    col = lax.broadcasted_iota(jnp.int32, (C, C), 1)

    def chunk(ci, carry):
        r0 = pl.multiple_of(ci * C, C)
        q = _silu(q_ref[pl.ds(r0, C), :])
        f = lb + (1.0 - lb) * _sigmoid(f_ref[pl.ds(r0, C), :])
        k = 1.0 - f
        v = i_ref[pl.ds(r0, C), :].astype(BF16)
        z = _dot_01_f32(sm, jnp.log(f))
        b = z[:C]
        sfx = z[C:2 * C]
        st = st_ref[...]
        inter = _dot_nt((q * jnp.exp(b)).astype(BF16), st.astype(BF16))
        a = jnp.where(row == col, jnp.sum(q * k, axis=-1, keepdims=True), 0.0)
        for l in range(L):
            el = jnp.exp(z[(2 + l) * C:(3 + l) * C])
            al = _dot_nt((q * el).astype(BF16), (k * el).astype(BF16))
            same_block = (row >> (l + 1)) == (col >> (l + 1))
            upper_lower = (((row >> l) & 1) == 1) & (((col >> l) & 1) == 0)
            a = a + jnp.where(same_block & upper_lower, al, 0.0)
        o = inter + _dot(a.astype(BF16), v)
        st_ref[...] = jnp.exp(b[C - 1:C]) * st + _dot_tn(v, (k * jnp.exp(sfx)).astype(BF16))
        o = _rms(o, ng) * _silu(g_ref[pl.ds(r0, C), :])
        o_ref[pl.ds(r0, C), :] = o.astype(o_ref.dtype)
        return carry

    lax.fori_loop(0, tc // C, chunk, 0)


def _hgrn(proj, B, T, lb_param, norm_g, *, layer, tc):
    M, W4 = proj.shape
    W = W4 // 4
    H = W // HGRN_HEAD_DIM
    C = HGRN_CHUNK
    assert T % tc == 0 and tc % C == 0
    nt = T // tc
    sm = jnp.asarray(_hgrn_sum_matrix(C), BF16)
    hd = HGRN_HEAD_DIM
    blk = lambda off: pl.BlockSpec((tc, hd), lambda b, h, t, off=off: (b * nt + t, off * H + h))
    return pl.pallas_call(
        functools.partial(_hgrn_kernel, layer=layer, C=C),
        grid=(B, H, nt),
        in_specs=[blk(0), blk(1), blk(2), blk(3),
                  pl.BlockSpec((lb_param.shape[0], hd), lambda b, h, t: (0, h)),
                  pl.BlockSpec((1, hd), lambda b, h, t: (0, 0)),
                  pl.BlockSpec(sm.shape, lambda b, h, t: (0, 0))],
        out_specs=pl.BlockSpec((tc, hd), lambda b, h, t: (b * nt + t, h)),
        out_shape=jax.ShapeDtypeStruct((M, W), BF16),
        scratch_shapes=[pltpu.VMEM((hd, hd), F32)],
        compiler_params=pltpu.CompilerParams(dimension_semantics=("parallel", "parallel", "arbitrary")),
        name="hgrn2",
    )(proj, proj, proj, proj, lb_param, norm_g.reshape(1, hd), sm)


def _rope_table_kernel(pos_ref, cs_ref, sup_ref, sdn_ref):
    half = ROT_DIM // 2
    lane = lax.broadcasted_iota(jnp.int32, (1, LANES), 1)
    d = lane % SWA_HEAD_DIM
    inv_freq = jnp.exp((d % half).astype(F32) * (-math.log(ROPE_THETA) / half))
    ang = pos_ref[...] * inv_freq
    cos, sin = jnp.cos(ang), jnp.sin(ang)
    cs_ref[...] = jnp.where(d < ROT_DIM, cos, 1.0)
    sup_ref[...] = jnp.where(d < half, -sin, 0.0)
    sdn_ref[...] = jnp.where((d >= half) & (d < ROT_DIM), sin, 0.0)


def _rope_tables(positions):
    T = positions.shape[0]
    out = jax.ShapeDtypeStruct((T, LANES), F32)
    return pl.pallas_call(
        _rope_table_kernel,
        grid=(1,),
        in_specs=[pl.BlockSpec((T, 1), lambda i: (0, 0))],
        out_specs=[pl.BlockSpec((T, LANES), lambda i: (0, 0))] * 3,
        out_shape=[out, out, out],
        name="rope_tables",
    )(positions.astype(F32).reshape(T, 1))


def _rope(x, cs, sup, sdn):
    w = x.shape[1]
    rep = w // LANES
    tile = lambda a: a if rep == 1 else jnp.concatenate([a] * rep, axis=1)
    half = ROT_DIM // 2
    return x * tile(cs) + pltpu.roll(x, w - half, 1) * tile(sup) + pltpu.roll(x, half, 1) * tile(sdn)


def _swa_kernel(sink_ref, q_ref, kc_ref, kp_ref, vc_ref, vp_ref,
                csc_ref, supc_ref, sdnc_ref, csp_ref, supp_ref, sdnp_ref, o_ref, *, n_kv):
    n = pl.program_id(1)
    W = SWA_WINDOW
    G = SWA_GROUP
    lane = lax.broadcasted_iota(jnp.int32, (1, LANES), 1)
    lo_half = lane < SWA_HEAD_DIM

    q = _rope(q_ref[...], csc_ref[...], supc_ref[...], sdnc_ref[...]) * (SWA_HEAD_DIM ** -0.5)
    k = jnp.concatenate([_rope(kp_ref[...], csp_ref[...], supp_ref[...], sdnp_ref[...]),
                         _rope(kc_ref[...], csc_ref[...], supc_ref[...], sdnc_ref[...])], axis=0)
    v = jnp.concatenate([vp_ref[...], vc_ref[...]], axis=0)

    qi = lax.broadcasted_iota(jnp.int32, (W, 2 * W), 0)
    kj = lax.broadcasted_iota(jnp.int32, (W, 2 * W), 1)
    diff = qi + W - kj
    allowed = (diff >= 0) & (diff < W) & ((kj >= W) | (n > 0))
    allowed = jnp.concatenate([allowed] * (G // 2), axis=0)
    half_mask = [lo_half, jnp.logical_not(lo_half)]

    for j in range(n_kv):
        pair = slice((j // 2) * LANES, (j // 2 + 1) * LANES)
        kpair, vpair = k[:, pair], v[:, pair]
        k_half = [kpair, pltpu.roll(kpair, SWA_HEAD_DIM, 1)]
        v_half = [vpair, pltpu.roll(vpair, SWA_HEAD_DIM, 1)]
        if j % 2 == 1:
            k_half.reverse()
            v_half.reverse()
        outs = []
        for hh in range(2):
            heads = [j * G + g for g in range(hh, G, 2)]
            qs = jnp.concatenate(
                [jnp.where(half_mask[hh], q[:, (h // 2) * LANES:(h // 2 + 1) * LANES], 0.0) for h in heads],
                axis=0).astype(BF16)
            s = _dot_nt(qs, k_half[hh].astype(BF16))
            s = jnp.where(allowed, s, -jnp.inf)
            sink = jnp.concatenate([jnp.full((W, 1), sink_ref[h], F32) for h in heads], axis=0)
            m = jnp.maximum(jnp.max(s, axis=-1, keepdims=True), sink)
            e = jnp.exp(s - m)
            p = e / (jnp.sum(e, axis=-1, keepdims=True) + jnp.exp(sink - m))
            outs.append(_dot(p.astype(BF16), v_half[hh].astype(BF16)))
        for gp in range(G // 2):
            pr = j * (G // 2) + gp
            rows = slice(gp * W, (gp + 1) * W)
            o_ref[:, pr * LANES:(pr + 1) * LANES] = jnp.where(lo_half, outs[0][rows], outs[1][rows]).astype(o_ref.dtype)


def _swa(proj, B, T, sinks, tables, *, n_q, n_kv):
    M = proj.shape[0]
    W = SWA_WINDOW
    assert T % W == 0 and n_q == n_kv * SWA_GROUP and n_kv % 2 == 0
    nblk = T // W
    qw, kw = n_q * SWA_HEAD_DIM, n_kv * SWA_HEAD_DIM
    assert qw % kw == 0
    kcol = qw // kw
    cs, sup, sdn = tables
    cur = lambda b, n: (b * nblk + n, 0)
    tcur = lambda b, n: (n, 0)
    tprev = lambda b, n: (jnp.maximum(n - 1, 0), 0)
    tspec = lambda im: pl.BlockSpec((W, LANES), im)
    return pl.pallas_call(
        functools.partial(_swa_kernel, n_kv=n_kv),
        grid=(B, nblk),
        in_specs=[pl.BlockSpec(memory_space=pltpu.SMEM),
                  pl.BlockSpec((W, qw), cur),
                  pl.BlockSpec((W, kw), lambda b, n: (b * nblk + n, kcol)),
                  pl.BlockSpec((W, kw), lambda b, n: (b * nblk + jnp.maximum(n - 1, 0), kcol)),
                  pl.BlockSpec((W, kw), lambda b, n: (b * nblk + n, kcol + 1)),
                  pl.BlockSpec((W, kw), lambda b, n: (b * nblk + jnp.maximum(n - 1, 0), kcol + 1)),
                  tspec(tcur), tspec(tcur), tspec(tcur), tspec(tprev), tspec(tprev), tspec(tprev)],
        out_specs=pl.BlockSpec((W, qw), cur),
        out_shape=jax.ShapeDtypeStruct((M, qw), BF16),
        compiler_params=pltpu.CompilerParams(dimension_semantics=("parallel", "parallel")),
        name="swa",
    )(sinks, proj, proj, proj, proj, proj, cs, sup, sdn, cs, sup, sdn)


def _sconv_kernel(b_ref, c_ref, x_ref, w_ref, o_ref, carry_ref):
    tt = b_ref.shape[0]

    @pl.when(pl.program_id(1) == 0)
    def _():
        carry_ref[...] = jnp.zeros(carry_ref.shape, F32)

    u = c_ref[...] * x_ref[...]
    y = _causal_conv3(u, carry_ref[...], w_ref[...])
    carry_ref[...] = u[tt - SUBLANES:]
    o_ref[...] = (b_ref[...] * y).astype(o_ref.dtype)


def _sconv(proj, B, T, conv_w, *, tt):
    M, D3 = proj.shape
    D = D3 // 3
    assert T % tt == 0
    nt = T // tt
    blk = lambda off: pl.BlockSpec((tt, D), lambda b, t, off=off: (b * nt + t, off))
    return pl.pallas_call(
        _sconv_kernel,
        grid=(B, nt),
        in_specs=[blk(0), blk(1), blk(2), pl.BlockSpec(conv_w.shape, lambda b, t: (0, 0))],
        out_specs=pl.BlockSpec((tt, D), lambda b, t: (b * nt + t, 0)),
        out_shape=jax.ShapeDtypeStruct((M, D), BF16),
        scratch_shapes=[pltpu.VMEM((SUBLANES, D), F32)],
        compiler_params=pltpu.CompilerParams(dimension_semantics=("parallel", "arbitrary"),
                                             vmem_limit_bytes=_vmem_limit(12 * _nbytes((tt, D), F32), 4 * 2**20)),
        name="sconv",
    )(proj, proj, proj, conv_w)


def _fox_cum_kernel(f_ref, bf_ref, tri_ref, ct_ref, carry_ref):
    tb = f_ref.shape[0]

    @pl.when(pl.program_id(1) == 0)
    def _():
        carry_ref[...] = jnp.zeros(carry_ref.shape, F32)

    x = f_ref[...] + bf_ref[...]
    log_f = jnp.minimum(x, 0.0) - jnp.log(1.0 + jnp.exp(-jnp.abs(x)))
    c = _dot_01_f32(tri_ref[...], log_f) + carry_ref[...]
    carry_ref[...] = c[tb - 1:tb]
    ct_ref[0] = c.T


def _fox_cum(f_logit, B, T, b_f, *, tb):
    assert T % tb == 0
    nt = T // tb
    tri = jnp.asarray(np.tril(np.ones((tb, tb), np.float32)), BF16)
    return pl.pallas_call(
        _fox_cum_kernel,
        grid=(B, nt),
        in_specs=[pl.BlockSpec((tb, LANES), lambda b, t: (b * nt + t, 0)),
                  pl.BlockSpec((1, LANES), lambda b, t: (0, 0)),
                  pl.BlockSpec((tb, tb), lambda b, t: (0, 0))],
        out_specs=pl.BlockSpec((1, LANES, tb), lambda b, t: (b, 0, t)),
        out_shape=jax.ShapeDtypeStruct((B, LANES, T), F32),
        scratch_shapes=[pltpu.VMEM((1, LANES), F32)],
        compiler_params=pltpu.CompilerParams(dimension_semantics=("parallel", "arbitrary")),
        name="fox_cum",
    )(f_logit, b_f.reshape(1, LANES), tri)


def _fox_kernel(q_ref, k_ref, v_ref, g_ref, ct_ref, o_ref, acc_ref, m_ref, l_ref, *, tq, tk):
    qi = pl.program_id(2)
    lane = lax.broadcasted_iota(jnp.int32, (1, LANES), 1)
    lo_half = lane < FOX_HEAD_DIM
    q2 = q_ref[...] * (FOX_HEAD_DIM ** -0.5)
    row = lax.broadcasted_iota(jnp.int32, (tq, tk), 0)
    col = lax.broadcasted_iota(jnp.int32, (tq, tk), 1)
    n_full = qi * (tq // tk)
    outs = []
    for hh in range(2):
        qm = jnp.where(lo_half if hh == 0 else jnp.logical_not(lo_half), q2, 0.0).astype(BF16)
        m_ref[...] = jnp.full(m_ref.shape, -jnp.inf, F32)
        l_ref[...] = jnp.zeros(l_ref.shape, F32)
        acc_ref[...] = jnp.zeros(acc_ref.shape, F32)

        def step(j, masked):
            r0 = pl.multiple_of(j * tk, tk)
            s = _dot_nt(qm, k_ref[pl.ds(r0, tk), :].astype(BF16))
            s = s - ct_ref[0, 0, j][hh:hh + 1, :]
            if masked:
                s = jnp.where(col + j * tk <= row + qi * tq, s, -jnp.inf)
            m_old = m_ref[...]
            m_new = jnp.maximum(m_old, jnp.max(s, axis=-1, keepdims=True))
            alpha = jnp.exp(m_old - m_new)
            p = jnp.exp(s - m_new)
            l_ref[...] = alpha * l_ref[...] + jnp.sum(p, axis=-1, keepdims=True)
            acc_ref[...] = alpha * acc_ref[...] + _dot(p.astype(BF16), v_ref[pl.ds(r0, tk), :].astype(BF16))
            m_ref[...] = m_new

        def full_step(j, carry):
            step(j, False)
            return carry

        lax.fori_loop(0, n_full, full_step, 0)
        for d in range(tq // tk):
            step(n_full + d, True)
        outs.append(acc_ref[...] / l_ref[...])
    o = jnp.where(lo_half, outs[0], outs[1]) * _sigmoid(g_ref[...])
    o_ref[...] = o.astype(o_ref.dtype)


def _fox_attn(qkvg, B, T, ct, *, n_heads, tq, tk):
    M, W4 = qkvg.shape
    W = W4 // 4
    assert W == n_heads * FOX_HEAD_DIM and T % tq == 0 and tq % tk == 0 and n_heads % 2 == 0
    hp = n_heads // 2
    nq, nk = T // tq, T // tk
    return pl.pallas_call(
        functools.partial(_fox_kernel, tq=tq, tk=tk),
        grid=(B, hp, nq),
        in_specs=[pl.BlockSpec((tq, LANES), lambda b, h, i: (b * nq + i, h)),
                  pl.BlockSpec((T, LANES), lambda b, h, i: (b, hp + h)),
                  pl.BlockSpec((T, LANES), lambda b, h, i: (b, 2 * hp + h)),
                  pl.BlockSpec((tq, LANES), lambda b, h, i: (b * nq + i, 3 * hp + h)),
                  pl.BlockSpec((1, 1, nk, 2, tk), lambda b, h, i: (b, h, 0, 0, 0))],
        out_specs=pl.BlockSpec((tq, LANES), lambda b, h, i: (b * nq + i, h)),
        out_shape=jax.ShapeDtypeStruct((M, W), BF16),
        scratch_shapes=[pltpu.VMEM((tq, LANES), F32), pltpu.VMEM((tq, 1), F32), pltpu.VMEM((tq, 1), F32)],
        compiler_params=pltpu.CompilerParams(dimension_semantics=("parallel", "parallel", "arbitrary"),
                                             vmem_limit_bytes=_vmem_limit(8 * _nbytes((T, LANES), F32),
                                                                          12 * _nbytes((tq, tk), F32), 8 * 2**20)),
        name="fox_attn",
    )(qkvg, qkvg, qkvg, qkvg, ct)


def _pick(n, prefs):
    for t in prefs:
        if n % t == 0:
            return t
    raise ValueError(f"no tile in {prefs} divides {n}")


def kernel(x, positions, mix_pre_g, mix_post_g, ffn_pre_g, ffn_post_g, hgrn_w_in, hgrn_w_out, hgrn_norm_g, hgrn_lb_param, swa_w_in, swa_w_out, swa_sinks, sc_w_in, sc_conv_w, sc_w_out, fox_w_in, fox_b_f, fox_w_out, ffn_w_up, ffn_conv_w, ffn_conv_b, ffn_w_down):
    B, T, D = x.shape
    M = B * T
    depth = mix_pre_g.shape[0]
    xf = x.reshape(M, D)
    tm_proj = _pick(M, (1024, 512, 256, 128))
    tm_out = _pick(M, (512, 256, 128))
    tm_ffn = _pick(T, (512, 256, 128))
    bf = lambda w: w.astype(BF16)
    tables = None

    for i in range(depth):
        m, j = i % N_MIXERS, i // N_MIXERS
        if m == 0:
            proj = _norm_matmul(xf, mix_pre_g[i], bf(hgrn_w_in[j]), tm=tm_proj, tn=512)
            a = _hgrn(proj, B, T, hgrn_lb_param, hgrn_norm_g[j], layer=i, tc=_pick(T, (512, 256, 128, 64)))
            w_out = hgrn_w_out[j]
        elif m == 1:
            n_q = swa_sinks.shape[1]
            n_kv = n_q // SWA_GROUP
            if tables is None:
                tables = _rope_tables(positions)
            proj = _norm_matmul(xf, mix_pre_g[i], bf(swa_w_in[j]), tm=tm_proj, tn=512)
            a = _swa(proj, B, T, swa_sinks[j], tables, n_q=n_q, n_kv=n_kv)
            w_out = swa_w_out[j]
        elif m == 2:
            proj = _norm_matmul(xf, mix_pre_g[i], bf(sc_w_in[j]), tm=tm_proj, tn=512)
            a = _sconv(proj, B, T, sc_conv_w[j], tt=_pick(T, (256, 128)))
            w_out = sc_w_out[j]
        else:
            n_heads = fox_b_f.shape[1]
            W = n_heads * FOX_HEAD_DIM
            w_in = fox_w_in[j]
            w_qkvg = bf(jnp.concatenate([w_in[:, :3 * W], w_in[:, 3 * W + n_heads:]], axis=1))
            w_f = bf(jnp.pad(w_in[:, 3 * W:3 * W + n_heads], ((0, 0), (0, LANES - n_heads))))
            b_f = jnp.pad(fox_b_f[j], (0, LANES - n_heads))
            qkvg = _norm_matmul(xf, mix_pre_g[i], w_qkvg, tm=tm_proj, tn=512)
            f_logit = _norm_matmul(xf, mix_pre_g[i], w_f, tm=tm_proj, tn=LANES)
            tk = _pick(T, (512, 256, 128))
            ct = _fox_cum(f_logit, B, T, b_f, tb=_pick(T, (256, 128)))
            ct = ct[:, :n_heads].reshape(B, n_heads // 2, 2, T // tk, tk).transpose(0, 1, 3, 2, 4)
            a = _fox_attn(qkvg, B, T, ct, n_heads=n_heads, tq=tk, tk=tk)
            w_out = fox_w_out[j]
        xf = _out_proj(a, bf(w_out), mix_post_g[i], xf, tm=tm_out)
        xf = _ffn(xf, B, T, ffn_pre_g[i], bf(ffn_w_up[i]), ffn_conv_w[i], ffn_conv_b[i], bf(ffn_w_down[i]),
                  ffn_post_g[i], tm=tm_ffn, tf=_pick(ffn_w_down.shape[1], (512, 256, 128)))
    return xf.reshape(B, T, D)
```

```python
import functools
import math

import numpy as np
import jax
import jax.numpy as jnp
from jax import lax
from jax.experimental import pallas as pl
from jax.experimental.pallas import tpu as pltpu

F32 = jnp.float32
BF16 = jnp.bfloat16

RMS_EPS = 1e-6
N_MIXERS = 4
HGRN_HEAD_DIM = 128
HGRN_CHUNK = 64
SWA_HEAD_DIM = 64
SWA_GROUP = 8
SWA_WINDOW = 128
ROPE_THETA = 500000.0
ROT_DIM = 16
FOX_HEAD_DIM = 64
FFN_CONV_WIDTH = 3
LOG2E = math.log2(math.e)

LANES = 128
SUBLANES = 8
V7X_VMEM_BYTES = 64 * 2**20
VMEM_CAP_BYTES = 56 * 2**20


def _vmem_limit(*nbytes):
    return int(min(VMEM_CAP_BYTES, sum(nbytes)))


def _nbytes(shape, dtype):
    return int(np.prod(shape)) * jnp.dtype(dtype).itemsize


def _rms(x, g):
    ms = jnp.mean(x * x, axis=-1, keepdims=True)
    return x * lax.rsqrt(ms + RMS_EPS) * g


def _dot(a, b):
    return jnp.dot(a, b, preferred_element_type=F32)


def _dot_nt(a, b):
    return lax.dot_general(a, b, (((1,), (1,)), ((), ())), preferred_element_type=F32)


def _dot_tn(a, b):
    return lax.dot_general(a, b, (((0,), (0,)), ((), ())), preferred_element_type=F32)


def _dot_01_f32(m01, x):
    n = x.shape[1]
    hi = x.astype(BF16)
    r1 = x - hi.astype(F32)
    mid = r1.astype(BF16)
    lo = (r1 - mid.astype(F32)).astype(BF16)
    y = _dot(m01, jnp.concatenate([hi, mid, lo], axis=1))
    return (y[:, :n] + y[:, n:2 * n]) + y[:, 2 * n:]


def _sigmoid(x):
    return 1.0 / (1.0 + jnp.exp(-x))


def _silu(x):
    return x * _sigmoid(x)


def _causal_conv3(u, prev8, w):
    w0, w1, w2 = w[0:1], w[1:2], w[2:3]
    r1 = pltpu.roll(u, 1, 0)
    r2 = pltpu.roll(u, 2, 0)
    y = w2 * u + w1 * r1 + w0 * r2
    rows = lax.broadcasted_iota(jnp.int32, prev8.shape, 0)
    h1 = jnp.where(rows < 1, pltpu.roll(prev8, 1, 0), r1[:SUBLANES])
    h2 = jnp.where(rows < 2, pltpu.roll(prev8, 2, 0), r2[:SUBLANES])
    yh = w2 * u[:SUBLANES] + w1 * h1 + w0 * h2
    return jnp.concatenate([yh, y[SUBLANES:]], axis=0)


def _norm_matmul_kernel(x_ref, g_ref, w_ref, o_ref, hn_ref):
    @pl.when(pl.program_id(1) == 0)
    def _():
        hn_ref[...] = _rms(x_ref[...], g_ref[...]).astype(BF16)

    o_ref[...] = _dot(hn_ref[...], w_ref[...]).astype(o_ref.dtype)


def _norm_matmul(x, g, w, *, tm, tn, out_dtype=F32):
    M, D = x.shape
    N = w.shape[1]
    assert M % tm == 0 and N % tn == 0
    limit = _vmem_limit(2 * _nbytes((tm, D), F32), _nbytes((tm, D), BF16), 2 * _nbytes((D, tn), BF16),
                        3 * _nbytes((tm, tn), F32), _nbytes((tm, D), F32), 4 * 2**20)
    return pl.pallas_call(
        _norm_matmul_kernel,
        grid=(M // tm, N // tn),
        in_specs=[pl.BlockSpec((tm, D), lambda i, j: (i, 0)),
                  pl.BlockSpec((1, D), lambda i, j: (0, 0)),
                  pl.BlockSpec((D, tn), lambda i, j: (0, j))],
        out_specs=pl.BlockSpec((tm, tn), lambda i, j: (i, j)),
        out_shape=jax.ShapeDtypeStruct((M, N), out_dtype),
        scratch_shapes=[pltpu.VMEM((tm, D), BF16)],
        compiler_params=pltpu.CompilerParams(dimension_semantics=("parallel", "arbitrary"),
                                             vmem_limit_bytes=limit),
        name="norm_matmul",
    )(x, g.reshape(1, D), w)


def _out_proj_kernel(a_ref, w_ref, g_ref, x_ref, o_ref):
    y = _dot(a_ref[...], w_ref[...])
    o_ref[...] = x_ref[...] + _rms(y, g_ref[...])


def _out_proj(a, w, g, x, *, tm):
    M, K = a.shape
    D = w.shape[1]
    assert M % tm == 0
    limit = _vmem_limit(2 * _nbytes((tm, K), BF16), 2 * _nbytes((K, D), BF16), 4 * _nbytes((tm, D), F32),
                        2 * _nbytes((tm, D), F32), 4 * 2**20)
    return pl.pallas_call(
        _out_proj_kernel,
        grid=(M // tm,),
        in_specs=[pl.BlockSpec((tm, K), lambda i: (i, 0)),
                  pl.BlockSpec((K, D), lambda i: (0, 0)),
                  pl.BlockSpec((1, D), lambda i: (0, 0)),
                  pl.BlockSpec((tm, D), lambda i: (i, 0))],
        out_specs=pl.BlockSpec((tm, D), lambda i: (i, 0)),
        out_shape=jax.ShapeDtypeStruct((M, D), F32),
        compiler_params=pltpu.CompilerParams(dimension_semantics=("parallel",), vmem_limit_bytes=limit),
        name="out_proj",
    )(a, w, g.reshape(1, D), x)


def _ffn_kernel(x_ref, gpre_ref, wg_ref, wu_ref, cwg_ref, cwu_ref, cbg_ref, cbu_ref, wd_ref, gpost_ref,
                o_ref, hn_ref, acc_ref, carry_ref):
    ti = pl.program_id(1)
    c = pl.program_id(2)
    nc = pl.num_programs(2)
    tm = x_ref.shape[0]

    @pl.when(c == 0)
    def _():
        hn_ref[...] = _rms(x_ref[...], gpre_ref[...]).astype(BF16)
        acc_ref[...] = jnp.zeros(acc_ref.shape, F32)

    @pl.when(ti == 0)
    def _():
        carry_ref[c] = jnp.zeros(carry_ref.shape[1:], F32)

    hn = hn_ref[...]
    ug = _dot(hn, wg_ref[...])
    uu = _dot(hn, wu_ref[...])
    yg = _causal_conv3(ug, carry_ref[c, 0], cwg_ref[...]) + cbg_ref[...]
    yu = _causal_conv3(uu, carry_ref[c, 1], cwu_ref[...]) + cbu_ref[...]
    carry_ref[c, 0] = ug[tm - SUBLANES:]
    carry_ref[c, 1] = uu[tm - SUBLANES:]
    act = (_silu(yg) * yu).astype(BF16)
    acc_ref[...] += _dot(act, wd_ref[...])

    @pl.when(c == nc - 1)
    def _():
        o_ref[...] = x_ref[...] + _rms(acc_ref[...], gpost_ref[...])


def _ffn(x, B, T, g_pre, w_up, conv_w, conv_b, w_down, g_post, *, tm, tf):
    M, D = x.shape
    F = w_down.shape[0]
    assert T % tm == 0 and F % tf == 0 and tm % SUBLANES == 0
    nt, nc = T // tm, F // tf
    conv_b = conv_b.reshape(1, 2 * F)
    limit = _vmem_limit(4 * _nbytes((tm, D), F32), _nbytes((tm, D), BF16), _nbytes((tm, D), F32),
                        4 * _nbytes((D, tf), BF16), 2 * _nbytes((tf, D), BF16),
                        8 * _nbytes((tm, tf), F32), _nbytes((tm, D), F32), 4 * 2**20)
    return pl.pallas_call(
        _ffn_kernel,
        grid=(B, nt, nc),
        in_specs=[pl.BlockSpec((tm, D), lambda b, t, c: (b * nt + t, 0)),
                  pl.BlockSpec((1, D), lambda b, t, c: (0, 0)),
                  pl.BlockSpec((D, tf), lambda b, t, c: (0, c)),
                  pl.BlockSpec((D, tf), lambda b, t, c: (0, nc + c)),
                  pl.BlockSpec((FFN_CONV_WIDTH, tf), lambda b, t, c: (0, c)),
                  pl.BlockSpec((FFN_CONV_WIDTH, tf), lambda b, t, c: (0, nc + c)),
                  pl.BlockSpec((1, tf), lambda b, t, c: (0, c)),
                  pl.BlockSpec((1, tf), lambda b, t, c: (0, nc + c)),
                  pl.BlockSpec((tf, D), lambda b, t, c: (c, 0)),
                  pl.BlockSpec((1, D), lambda b, t, c: (0, 0))],
        out_specs=pl.BlockSpec((tm, D), lambda b, t, c: (b * nt + t, 0)),
        out_shape=jax.ShapeDtypeStruct((M, D), F32),
        scratch_shapes=[pltpu.VMEM((tm, D), BF16),
                        pltpu.VMEM((tm, D), F32),
                        pltpu.VMEM((nc, 2, SUBLANES, tf), F32)],
        compiler_params=pltpu.CompilerParams(dimension_semantics=("parallel", "arbitrary", "arbitrary"),
                                             vmem_limit_bytes=limit),
        name="ffn",
    )(x, g_pre.reshape(1, D), w_up, w_up, conv_w, conv_w, conv_b, conv_b, w_down, g_post.reshape(1, D))


def _hgrn_sum_matrix(C):
    L = int(math.log2(C))
    t = np.arange(C)[:, None]
    j = np.arange(C)[None, :]
    blocks = [j <= t, j > t]
    for l in range(L):
        half, blk = 1 << l, 2 << l
        mid = t - t % blk + half - 1
        is_query = (t % blk) >= half
        blocks.append(np.where(is_query, (j > mid) & (j <= t), (j > t) & (j <= mid)))
    return np.concatenate(blocks, axis=0).astype(np.float32)


def _hgrn_kernel(q_ref, f_ref, i_ref, g_ref, lbp_ref, ng_ref, sm_ref, o_ref, st_ref, *, layer, C, NH):
    tc = q_ref.shape[0]
    L = int(math.log2(C))
    hd = HGRN_HEAD_DIM

    @pl.when(pl.program_id(2) == 0)
    def _():
        st_ref[...] = jnp.zeros(st_ref.shape, F32)

    p = lbp_ref[...]
    e = jnp.exp(p - jnp.max(p, axis=0, keepdims=True))
    lb_all = jnp.sum(e[:layer + 1], axis=0, keepdims=True) / jnp.sum(e, axis=0, keepdims=True)
    ng = ng_ref[...]
    sm = sm_ref[...]

    row = lax.broadcasted_iota(jnp.int32, (C, C), 0)
    col = lax.broadcasted_iota(jnp.int32, (C, C), 1)
    level_mask = [((row >> (l + 1)) == (col >> (l + 1))) & (((row >> l) & 1) == 1) & (((col >> l) & 1) == 0)
                  for l in range(L)]

    def chunk(ci, carry):
        r0 = pl.multiple_of(ci * C, C)
        for hh in range(NH):
            cols = slice(hh * hd, (hh + 1) * hd)
            lb = lb_all[:, cols]
            q = _silu(q_ref[pl.ds(r0, C), cols])
            f = lb + (1.0 - lb) * _sigmoid(f_ref[pl.ds(r0, C), cols])
            k = 1.0 - f
            v = i_ref[pl.ds(r0, C), cols].astype(BF16)
            z = _dot_01_f32(sm, jnp.log(f))
            b = z[:C]
            sfx = z[C:2 * C]
            st = st_ref[hh]
            inter = _dot_nt((q * jnp.exp(b)).astype(BF16), st.astype(BF16))
            a = jnp.where(row == col, jnp.sum(q * k, axis=-1, keepdims=True), 0.0)
            for l in range(L):
                el = jnp.exp(z[(2 + l) * C:(3 + l) * C])
                al = _dot_nt((q * el).astype(BF16), (k * el).astype(BF16))
                a = a + jnp.where(level_mask[l], al, 0.0)
            o = inter + _dot(a.astype(BF16), v)
            st_ref[hh] = jnp.exp(b[C - 1:C]) * st + _dot_tn(v, (k * jnp.exp(sfx)).astype(BF16))
            o = _rms(o, ng) * _silu(g_ref[pl.ds(r0, C), cols])
            o_ref[pl.ds(r0, C), cols] = o.astype(o_ref.dtype)
        return carry

    lax.fori_loop(0, tc // C, chunk, 0, unroll=2)


def _hgrn(proj, B, T, lb_param, norm_g, *, layer, tc, heads_per_step):
    M, W4 = proj.shape
    W = W4 // 4
    H = W // HGRN_HEAD_DIM
    C = HGRN_CHUNK
    NH = heads_per_step
    assert T % tc == 0 and tc % C == 0 and H % NH == 0
    nt = T // tc
    HG = H // NH
    sm = jnp.asarray(_hgrn_sum_matrix(C), BF16)
    hd = HGRN_HEAD_DIM
    bw = NH * hd
    blk = lambda off: pl.BlockSpec((tc, bw), lambda b, h, t, off=off: (b * nt + t, off * HG + h))
    return pl.pallas_call(
        functools.partial(_hgrn_kernel, layer=layer, C=C, NH=NH),
        grid=(B, HG, nt),
        in_specs=[blk(0), blk(1), blk(2), blk(3),
                  pl.BlockSpec((lb_param.shape[0], bw), lambda b, h, t: (0, h)),
                  pl.BlockSpec((1, hd), lambda b, h, t: (0, 0)),
                  pl.BlockSpec(sm.shape, lambda b, h, t: (0, 0))],
        out_specs=pl.BlockSpec((tc, bw), lambda b, h, t: (b * nt + t, h)),
        out_shape=jax.ShapeDtypeStruct((M, W), BF16),
        scratch_shapes=[pltpu.VMEM((NH, hd, hd), F32)],
        compiler_params=pltpu.CompilerParams(dimension_semantics=("parallel", "parallel", "arbitrary")),
        name="hgrn2",
    )(proj, proj, proj, proj, lb_param, norm_g.reshape(1, hd), sm)


def _rope_table_kernel(pos_ref, cs_ref, sup_ref, sdn_ref):
    half = ROT_DIM // 2
    lane = lax.broadcasted_iota(jnp.int32, (1, LANES), 1)
    d = lane % SWA_HEAD_DIM
    inv_freq = jnp.exp((d % half).astype(F32) * (-math.log(ROPE_THETA) / half))
    ang = pos_ref[...] * inv_freq
    cos, sin = jnp.cos(ang), jnp.sin(ang)
    cs_ref[...] = jnp.where(d < ROT_DIM, cos, 1.0)
    sup_ref[...] = jnp.where(d < half, -sin, 0.0)
    sdn_ref[...] = jnp.where((d >= half) & (d < ROT_DIM), sin, 0.0)


def _rope_tables(positions):
    T = positions.shape[0]
    out = jax.ShapeDtypeStruct((T, LANES), F32)
    return pl.pallas_call(
        _rope_table_kernel,
        grid=(1,),
        in_specs=[pl.BlockSpec((T, 1), lambda i: (0, 0))],
        out_specs=[pl.BlockSpec((T, LANES), lambda i: (0, 0))] * 3,
        out_shape=[out, out, out],
        name="rope_tables",
    )(positions.astype(F32).reshape(T, 1))


def _rope(x, cs, sup, sdn):
    w = x.shape[1]
    rep = w // LANES
    tile = lambda a: a if rep == 1 else jnp.concatenate([a] * rep, axis=1)
    half = ROT_DIM // 2
    return x * tile(cs) + pltpu.roll(x, w - half, 1) * tile(sup) + pltpu.roll(x, half, 1) * tile(sdn)


def _swa_kernel(sink_ref, q_ref, kc_ref, kp_ref, vc_ref, vp_ref,
                csc_ref, supc_ref, sdnc_ref, csp_ref, supp_ref, sdnp_ref, o_ref, *, n_kv):
    n = pl.program_id(1)
    W = SWA_WINDOW
    G = SWA_GROUP
    lane = lax.broadcasted_iota(jnp.int32, (1, LANES), 1)
    lo_half = lane < SWA_HEAD_DIM

    q = _rope(q_ref[...], csc_ref[...], supc_ref[...], sdnc_ref[...]) * (SWA_HEAD_DIM ** -0.5)
    k = jnp.concatenate([_rope(kp_ref[...], csp_ref[...], supp_ref[...], sdnp_ref[...]),
                         _rope(kc_ref[...], csc_ref[...], supc_ref[...], sdnc_ref[...])], axis=0)
    v = jnp.concatenate([vp_ref[...], vc_ref[...]], axis=0)

    qi = lax.broadcasted_iota(jnp.int32, (W, 2 * W), 0)
    kj = lax.broadcasted_iota(jnp.int32, (W, 2 * W), 1)
    diff = qi + W - kj
    allowed = (diff >= 0) & (diff < W) & ((kj >= W) | (n > 0))
    allowed = jnp.concatenate([allowed] * (G // 2), axis=0)
    half_mask = [lo_half, jnp.logical_not(lo_half)]

    for j in range(n_kv):
        pair = slice((j // 2) * LANES, (j // 2 + 1) * LANES)
        kpair, vpair = k[:, pair], v[:, pair]
        k_half = [kpair, pltpu.roll(kpair, SWA_HEAD_DIM, 1)]
        v_half = [vpair, pltpu.roll(vpair, SWA_HEAD_DIM, 1)]
        if j % 2 == 1:
            k_half.reverse()
            v_half.reverse()
        outs = []
        for hh in range(2):
            heads = [j * G + g for g in range(hh, G, 2)]
            qs = jnp.concatenate(
                [jnp.where(half_mask[hh], q[:, (h // 2) * LANES:(h // 2 + 1) * LANES], 0.0) for h in heads],
                axis=0).astype(BF16)
            s = _dot_nt(qs, k_half[hh].astype(BF16))
            s = jnp.where(allowed, s, -jnp.inf)
            sink = jnp.concatenate([jnp.full((W, 1), sink_ref[h], F32) for h in heads], axis=0)
            m = jnp.maximum(jnp.max(s, axis=-1, keepdims=True), sink)
            e = jnp.exp(s - m)
            p = e / (jnp.sum(e, axis=-1, keepdims=True) + jnp.exp(sink - m))
            outs.append(_dot(p.astype(BF16), v_half[hh].astype(BF16)))
        for gp in range(G // 2):
            pr = j * (G // 2) + gp
            rows = slice(gp * W, (gp + 1) * W)
            o_ref[:, pr * LANES:(pr + 1) * LANES] = jnp.where(lo_half, outs[0][rows], outs[1][rows]).astype(o_ref.dtype)


def _swa(proj, B, T, sinks, tables, *, n_q, n_kv):
    M = proj.shape[0]
    W = SWA_WINDOW
    assert T % W == 0 and n_q == n_kv * SWA_GROUP and n_kv % 2 == 0
    nblk = T // W
    qw, kw = n_q * SWA_HEAD_DIM, n_kv * SWA_HEAD_DIM
    assert qw % kw == 0
    kcol = qw // kw
    cs, sup, sdn = tables
    cur = lambda b, n: (b * nblk + n, 0)
    tcur = lambda b, n: (n, 0)
    tprev = lambda b, n: (jnp.maximum(n - 1, 0), 0)
    tspec = lambda im: pl.BlockSpec((W, LANES), im)
    return pl.pallas_call(
        functools.partial(_swa_kernel, n_kv=n_kv),
        grid=(B, nblk),
        in_specs=[pl.BlockSpec(memory_space=pltpu.SMEM),
                  pl.BlockSpec((W, qw), cur),
                  pl.BlockSpec((W, kw), lambda b, n: (b * nblk + n, kcol)),
                  pl.BlockSpec((W, kw), lambda b, n: (b * nblk + jnp.maximum(n - 1, 0), kcol)),
                  pl.BlockSpec((W, kw), lambda b, n: (b * nblk + n, kcol + 1)),
                  pl.BlockSpec((W, kw), lambda b, n: (b * nblk + jnp.maximum(n - 1, 0), kcol + 1)),
                  tspec(tcur), tspec(tcur), tspec(tcur), tspec(tprev), tspec(tprev), tspec(tprev)],
        out_specs=pl.BlockSpec((W, qw), cur),
        out_shape=jax.ShapeDtypeStruct((M, qw), BF16),
        compiler_params=pltpu.CompilerParams(dimension_semantics=("parallel", "parallel")),
        name="swa",
    )(sinks, proj, proj, proj, proj, proj, cs, sup, sdn, cs, sup, sdn)


def _sconv_kernel(b_ref, c_ref, x_ref, w_ref, o_ref, carry_ref):
    tt = b_ref.shape[0]

    @pl.when(pl.program_id(1) == 0)
    def _():
        carry_ref[...] = jnp.zeros(carry_ref.shape, F32)

    u = c_ref[...] * x_ref[...]
    y = _causal_conv3(u, carry_ref[...], w_ref[...])
    carry_ref[...] = u[tt - SUBLANES:]
    o_ref[...] = (b_ref[...] * y).astype(o_ref.dtype)


def _sconv(proj, B, T, conv_w, *, tt):
    M, D3 = proj.shape
    D = D3 // 3
    assert T % tt == 0
    nt = T // tt
    blk = lambda off: pl.BlockSpec((tt, D), lambda b, t, off=off: (b * nt + t, off))
    return pl.pallas_call(
        _sconv_kernel,
        grid=(B, nt),
        in_specs=[blk(0), blk(1), blk(2), pl.BlockSpec(conv_w.shape, lambda b, t: (0, 0))],
        out_specs=pl.BlockSpec((tt, D), lambda b, t: (b * nt + t, 0)),
        out_shape=jax.ShapeDtypeStruct((M, D), BF16),
        scratch_shapes=[pltpu.VMEM((SUBLANES, D), F32)],
        compiler_params=pltpu.CompilerParams(dimension_semantics=("parallel", "arbitrary"),
                                             vmem_limit_bytes=_vmem_limit(12 * _nbytes((tt, D), F32), 4 * 2**20)),
        name="sconv",
    )(proj, proj, proj, conv_w)


def _fox_cum_kernel(f_ref, bf_ref, tri_ref, c_ref, carry_ref):
    tb = f_ref.shape[0]

    @pl.when(pl.program_id(1) == 0)
    def _():
        carry_ref[...] = jnp.zeros(carry_ref.shape, F32)

    x = f_ref[...] + bf_ref[...]
    log_f = jnp.minimum(x, 0.0) - jnp.log(1.0 + jnp.exp(-jnp.abs(x)))
    c = _dot_01_f32(tri_ref[...], log_f) + carry_ref[...]
    carry_ref[...] = c[tb - 1:tb]
    c_ref[...] = c * LOG2E


def _fox_cum(f_logit, B, T, b_f, *, tb):
    assert T % tb == 0
    nt = T // tb
    tri = jnp.asarray(np.tril(np.ones((tb, tb), np.float32)), BF16)
    return pl.pallas_call(
        _fox_cum_kernel,
        grid=(B, nt),
        in_specs=[pl.BlockSpec((tb, LANES), lambda b, t: (b * nt + t, 0)),
                  pl.BlockSpec((1, LANES), lambda b, t: (0, 0)),
                  pl.BlockSpec((tb, tb), lambda b, t: (0, 0))],
        out_specs=pl.BlockSpec((tb, LANES), lambda b, t: (b * nt + t, 0)),
        out_shape=jax.ShapeDtypeStruct((B * T, LANES), F32),
        scratch_shapes=[pltpu.VMEM((1, LANES), F32)],
        compiler_params=pltpu.CompilerParams(dimension_semantics=("parallel", "arbitrary")),
        name="fox_cum",
    )(f_logit, b_f.reshape(1, LANES), tri)


def _fox_kernel(q_ref, k_ref, v_ref, g_ref, c_ref, o_ref, kx_ref, vt_ref, acc_ref, m_ref, l_ref, s_ref, mb_ref,
                *, tq, tk):
    h = pl.program_id(1)
    qi = pl.program_id(2)
    nk = vt_ref.shape[0]
    hd = FOX_HEAD_DIM

    @pl.when(qi == 0)
    def _():
        r = lax.broadcasted_iota(jnp.int32, (LANES, LANES), 0)
        cc = lax.broadcasted_iota(jnp.int32, (LANES, LANES), 1)

        def sel(piece):
            hit = ((cc == piece) & (r == 2 * h)) | ((cc == piece + 3) & (r == 2 * h + 1))
            return jnp.where(hit, 1.0, 0.0).astype(BF16)

        s_hi, s_mid, s_lo = sel(0), sel(1), sel(2)
        for j in range(nk):
            rows = slice(j * tk, (j + 1) * tk)
            kx_ref[rows, :LANES] = k_ref[rows, :].astype(BF16)
            c = c_ref[rows, :]
            hi = c.astype(BF16)
            r1 = c - hi.astype(F32)
            mid = r1.astype(BF16)
            lo = (r1 - mid.astype(F32)).astype(BF16)
            kx_ref[rows, LANES:] = (_dot(hi, s_hi) + _dot(mid, s_mid) + _dot(lo, s_lo)).astype(BF16)
            vt_ref[j] = v_ref[rows, :].T.astype(BF16)

    lane = lax.broadcasted_iota(jnp.int32, (1, LANES), 1)
    lo_half = lane < hd
    q2 = q_ref[...] * (hd ** -0.5 * LOG2E)
    qx = []
    for hh in range(2):
        qm = jnp.where(lo_half if hh == 0 else jnp.logical_not(lo_half), q2, 0.0)
        minus_one = jnp.where((lane >= 3 * hh) & (lane < 3 * hh + 3), -1.0, 0.0)
        qx.append(jnp.concatenate([qm, jnp.broadcast_to(minus_one, (tq, LANES))], axis=1).astype(BF16))

    m_ref[...] = jnp.full(m_ref.shape, -jnp.inf, F32)
    l_ref[...] = jnp.zeros(l_ref.shape, F32)
    acc_ref[...] = jnp.zeros(acc_ref.shape, F32)
    key_i = lax.broadcasted_iota(jnp.int32, (tk, tq), 0)
    qry_i = lax.broadcasted_iota(jnp.int32, (tk, tq), 1)

    def produce(j, buf, diag):
        kxb = kx_ref[pl.ds(pl.multiple_of(j * tk, tk), tk), :]
        for hh in range(2):
            st = _dot_nt(kxb, qx[hh])
            if diag is not None:
                st = jnp.where(key_i + diag * tk <= qry_i, st, -jnp.inf)
            s_ref[buf, hh] = st
            mb_ref[buf, hh:hh + 1, :] = jnp.max(st, axis=0, keepdims=True)

    def consume(j, buf):
        vtb = vt_ref[j]
        for hh in range(2):
            m_old = m_ref[hh:hh + 1, :]
            m_new = jnp.maximum(m_old, mb_ref[buf, hh:hh + 1, :])
            alpha = jnp.exp2(m_old - m_new)
            p = jnp.exp2(s_ref[buf, hh] - m_new)
            l_ref[hh:hh + 1, :] = alpha * l_ref[hh:hh + 1, :] + jnp.sum(p, axis=0, keepdims=True)
            rows = slice(hh * hd, (hh + 1) * hd)
            acc_ref[rows, :] = alpha * acc_ref[rows, :] + _dot(vtb[rows, :], p.astype(BF16))
            m_ref[hh:hh + 1, :] = m_new

    assert tq == 2 * tk

    def pair(i, next_is_diagonal):
        produce(2 * i + 1, 1, None)
        consume(2 * i, 0)
        produce(2 * i + 2, 0, 0 if next_is_diagonal else None)
        consume(2 * i + 1, 1)

    def full_pair(i, carry):
        pair(i, False)
        return carry

    @pl.when(qi == 0)
    def _():
        produce(0, 0, 0)

    @pl.when(qi > 0)
    def _():
        produce(0, 0, None)
        lax.fori_loop(0, qi - 1, full_pair, 0)
        pair(qi - 1, True)

    produce(2 * qi + 1, 1, 1)
    consume(2 * qi, 0)
    consume(2 * qi + 1, 1)
    ot = jnp.concatenate([acc_ref[:hd, :] / l_ref[0:1, :], acc_ref[hd:, :] / l_ref[1:2, :]], axis=0)
    o_ref[...] = (ot.T * _sigmoid(g_ref[...])).astype(o_ref.dtype)


def _fox_attn(qkvg, B, T, c2, *, n_heads, tq, tk):
    M, W4 = qkvg.shape
    W = W4 // 4
    assert W == n_heads * FOX_HEAD_DIM and T % tq == 0 and tq % tk == 0 and n_heads % 2 == 0
    hp = n_heads // 2
    nq, nk = T // tq, T // tk
    limit = _vmem_limit(6 * _nbytes((T, LANES), F32), _nbytes((T, 2 * LANES), BF16), _nbytes((T, LANES), BF16),
                        16 * _nbytes((tk, tq), F32), 8 * 2**20)
    return pl.pallas_call(
        functools.partial(_fox_kernel, tq=tq, tk=tk),
        grid=(B, hp, nq),
        in_specs=[pl.BlockSpec((tq, LANES), lambda b, h, i: (b * nq + i, h)),
                  pl.BlockSpec((T, LANES), lambda b, h, i: (b, hp + h)),
                  pl.BlockSpec((T, LANES), lambda b, h, i: (b, 2 * hp + h)),
                  pl.BlockSpec((tq, LANES), lambda b, h, i: (b * nq + i, 3 * hp + h)),
                  pl.BlockSpec((T, LANES), lambda b, h, i: (b, 0))],
        out_specs=pl.BlockSpec((tq, LANES), lambda b, h, i: (b * nq + i, h)),
        out_shape=jax.ShapeDtypeStruct((M, W), BF16),
        scratch_shapes=[pltpu.VMEM((T, 2 * LANES), BF16), pltpu.VMEM((nk, LANES, tk), BF16),
                        pltpu.VMEM((LANES, tq), F32), pltpu.VMEM((2, tq), F32), pltpu.VMEM((2, tq), F32),
                        pltpu.VMEM((2, 2, tk, tq), F32), pltpu.VMEM((2, 2, tq), F32)],
        compiler_params=pltpu.CompilerParams(dimension_semantics=("parallel", "parallel", "arbitrary"),
                                             vmem_limit_bytes=limit),
        name="fox_attn",
    )(qkvg, qkvg, qkvg, qkvg, c2)


def _pick(n, prefs):
    for t in prefs:
        if n % t == 0:
            return t
    raise ValueError(f"no tile in {prefs} divides {n}")


def kernel(x, positions, mix_pre_g, mix_post_g, ffn_pre_g, ffn_post_g, hgrn_w_in, hgrn_w_out, hgrn_norm_g, hgrn_lb_param, swa_w_in, swa_w_out, swa_sinks, sc_w_in, sc_conv_w, sc_w_out, fox_w_in, fox_b_f, fox_w_out, ffn_w_up, ffn_conv_w, ffn_conv_b, ffn_w_down):
    B, T, D = x.shape
    M = B * T
    depth = mix_pre_g.shape[0]
    xf = x.reshape(M, D)
    tm_proj = _pick(M, (1024, 512, 256, 128))
    tm_out = _pick(M, (512, 256, 128))
    tm_ffn = _pick(T, (512, 256, 128))
    bf = lambda w: w.astype(BF16)
    tables = None

    for i in range(depth):
        m, j = i % N_MIXERS, i // N_MIXERS
        if m == 0:
            proj = _norm_matmul(xf, mix_pre_g[i], bf(hgrn_w_in[j]), tm=tm_proj, tn=512)
            a = _hgrn(proj, B, T, hgrn_lb_param, hgrn_norm_g[j], layer=i, tc=_pick(T, (512, 256, 128, 64)),
                      heads_per_step=4)
            w_out = hgrn_w_out[j]
        elif m == 1:
            n_q = swa_sinks.shape[1]
            n_kv = n_q // SWA_GROUP
            if tables is None:
                tables = _rope_tables(positions)
            proj = _norm_matmul(xf, mix_pre_g[i], bf(swa_w_in[j]), tm=tm_proj, tn=512)
            a = _swa(proj, B, T, swa_sinks[j], tables, n_q=n_q, n_kv=n_kv)
            w_out = swa_w_out[j]
        elif m == 2:
            proj = _norm_matmul(xf, mix_pre_g[i], bf(sc_w_in[j]), tm=tm_proj, tn=512)
            a = _sconv(proj, B, T, sc_conv_w[j], tt=_pick(T, (256, 128)))
            w_out = sc_w_out[j]
        else:
            n_heads = fox_b_f.shape[1]
            W = n_heads * FOX_HEAD_DIM
            w_in = fox_w_in[j]
            w_qkvg = bf(jnp.concatenate([w_in[:, :3 * W], w_in[:, 3 * W + n_heads:]], axis=1))
            w_f = bf(jnp.pad(w_in[:, 3 * W:3 * W + n_heads], ((0, 0), (0, LANES - n_heads))))
            b_f = jnp.pad(fox_b_f[j], (0, LANES - n_heads))
            qkvg = _norm_matmul(xf, mix_pre_g[i], w_qkvg, tm=tm_proj, tn=512)
            f_logit = _norm_matmul(xf, mix_pre_g[i], w_f, tm=tm_proj, tn=LANES)
            c2 = _fox_cum(f_logit, B, T, b_f, tb=_pick(T, (256, 128)))
            a = _fox_attn(qkvg, B, T, c2, n_heads=n_heads, tq=512, tk=256)
            w_out = fox_w_out[j]
        xf = _out_proj(a, bf(w_out), mix_post_g[i], xf, tm=tm_out)
        xf = _ffn(xf, B, T, ffn_pre_g[i], bf(ffn_w_up[i]), ffn_conv_w[i], ffn_conv_b[i], bf(ffn_w_down[i]),
                  ffn_post_g[i], tm=tm_ffn, tf=_pick(ffn_w_down.shape[1], (512, 256, 128)))
    return xf.reshape(B, T, D)
```

```python
import functools
import math

import numpy as np
import jax
import jax.numpy as jnp
from jax import lax
from jax.experimental import pallas as pl
from jax.experimental.pallas import tpu as pltpu

F32 = jnp.float32
BF16 = jnp.bfloat16

RMS_EPS = 1e-6
N_MIXERS = 4
HGRN_HEAD_DIM = 128
HGRN_CHUNK = 128
SWA_HEAD_DIM = 64
SWA_GROUP = 8
SWA_WINDOW = 128
ROPE_THETA = 500000.0
ROT_DIM = 16
FOX_HEAD_DIM = 64
FFN_CONV_WIDTH = 3
LOG2E = math.log2(math.e)

LANES = 128
SUBLANES = 8
V7X_VMEM_BYTES = 64 * 2**20
VMEM_CAP_BYTES = 56 * 2**20


def _vmem_limit(*nbytes):
    return int(min(VMEM_CAP_BYTES, sum(nbytes)))


def _nbytes(shape, dtype):
    return int(np.prod(shape)) * jnp.dtype(dtype).itemsize


def _rms(x, g):
    ms = jnp.mean(x * x, axis=-1, keepdims=True)
    return x * lax.rsqrt(ms + RMS_EPS) * g


def _dot(a, b):
    return jnp.dot(a, b, preferred_element_type=F32)


def _dot_nt(a, b):
    return lax.dot_general(a, b, (((1,), (1,)), ((), ())), preferred_element_type=F32)


def _dot_tn(a, b):
    return lax.dot_general(a, b, (((0,), (0,)), ((), ())), preferred_element_type=F32)


def _dot_01_f32(m01, x, pieces=3):
    n = x.shape[1]
    terms, rest = [], x
    for _ in range(pieces):
        t = rest.astype(BF16)
        terms.append(t)
        rest = rest - t.astype(F32)
    y = _dot(m01, jnp.concatenate(terms, axis=1))
    out = y[:, :n]
    for i in range(1, pieces):
        out = out + y[:, i * n:(i + 1) * n]
    return out


def _sigmoid(x):
    return 1.0 / (1.0 + jnp.exp(-x))


def _silu(x):
    return x * _sigmoid(x)


def _causal_conv3(u, prev8, w):
    w0, w1, w2 = w[0:1], w[1:2], w[2:3]
    r1 = pltpu.roll(u, 1, 0)
    r2 = pltpu.roll(u, 2, 0)
    y = w2 * u + w1 * r1 + w0 * r2
    rows = lax.broadcasted_iota(jnp.int32, prev8.shape, 0)
    h1 = jnp.where(rows < 1, pltpu.roll(prev8, 1, 0), r1[:SUBLANES])
    h2 = jnp.where(rows < 2, pltpu.roll(prev8, 2, 0), r2[:SUBLANES])
    yh = w2 * u[:SUBLANES] + w1 * h1 + w0 * h2
    return jnp.concatenate([yh, y[SUBLANES:]], axis=0)


def _norm_matmul_kernel(x_ref, g_ref, w_ref, o_ref, hn_ref):
    @pl.when(pl.program_id(1) == 0)
    def _():
        hn_ref[...] = _rms(x_ref[...], g_ref[...]).astype(BF16)

    o_ref[...] = _dot(hn_ref[...], w_ref[...]).astype(o_ref.dtype)


def _norm_matmul(x, g, w, *, tm, tn, out_dtype=F32):
    M, D = x.shape
    N = w.shape[1]
    if tn is None:
        tn = next(t for t in (1024, 1280, 768, 512, 256, LANES) if N % t == 0)
    assert M % tm == 0 and N % tn == 0
    limit = _vmem_limit(2 * _nbytes((tm, D), F32), _nbytes((tm, D), BF16), 2 * _nbytes((D, tn), BF16),
                        3 * _nbytes((tm, tn), F32), _nbytes((tm, D), F32), 4 * 2**20)
    return pl.pallas_call(
        _norm_matmul_kernel,
        grid=(M // tm, N // tn),
        in_specs=[pl.BlockSpec((tm, D), lambda i, j: (i, 0)),
                  pl.BlockSpec((1, D), lambda i, j: (0, 0)),
                  pl.BlockSpec((D, tn), lambda i, j: (0, j))],
        out_specs=pl.BlockSpec((tm, tn), lambda i, j: (i, j)),
        out_shape=jax.ShapeDtypeStruct((M, N), out_dtype),
        scratch_shapes=[pltpu.VMEM((tm, D), BF16)],
        compiler_params=pltpu.CompilerParams(dimension_semantics=("parallel", "arbitrary"),
                                             vmem_limit_bytes=limit),
        name="norm_matmul",
    )(x, g.reshape(1, D), w)


def _out_proj_kernel(a_ref, w_ref, g_ref, x_ref, o_ref):
    y = _dot(a_ref[...], w_ref[...])
    o_ref[...] = x_ref[...] + _rms(y, g_ref[...])


def _out_proj(a, w, g, x, *, tm):
    M, K = a.shape
    D = w.shape[1]
    assert M % tm == 0
    limit = _vmem_limit(2 * _nbytes((tm, K), BF16), 2 * _nbytes((K, D), BF16), 4 * _nbytes((tm, D), F32),
                        2 * _nbytes((tm, D), F32), 4 * 2**20)
    return pl.pallas_call(
        _out_proj_kernel,
        grid=(M // tm,),
        in_specs=[pl.BlockSpec((tm, K), lambda i: (i, 0)),
                  pl.BlockSpec((K, D), lambda i: (0, 0)),
                  pl.BlockSpec((1, D), lambda i: (0, 0)),
                  pl.BlockSpec((tm, D), lambda i: (i, 0))],
        out_specs=pl.BlockSpec((tm, D), lambda i: (i, 0)),
        out_shape=jax.ShapeDtypeStruct((M, D), F32),
        compiler_params=pltpu.CompilerParams(dimension_semantics=("parallel",), vmem_limit_bytes=limit),
        name="out_proj",
    )(a, w, g.reshape(1, D), x)


def _ffn_kernel(x_ref, gpre_ref, wg_ref, wu_ref, cwg_ref, cwu_ref, cbg_ref, cbu_ref, wd_ref, gpost_ref,
                o_ref, hn_ref, acc_ref, carry_ref):
    ti = pl.program_id(1)
    c = pl.program_id(2)
    nc = pl.num_programs(2)
    tm = x_ref.shape[0]

    @pl.when(c == 0)
    def _():
        hn_ref[...] = _rms(x_ref[...], gpre_ref[...]).astype(BF16)
        acc_ref[...] = jnp.zeros(acc_ref.shape, F32)

    @pl.when(ti == 0)
    def _():
        carry_ref[c] = jnp.zeros(carry_ref.shape[1:], F32)

    hn = hn_ref[...]
    ug = _dot(hn, wg_ref[...])
    uu = _dot(hn, wu_ref[...])
    yg = _causal_conv3(ug, carry_ref[c, 0], cwg_ref[...]) + cbg_ref[...]
    yu = _causal_conv3(uu, carry_ref[c, 1], cwu_ref[...]) + cbu_ref[...]
    carry_ref[c, 0] = ug[tm - SUBLANES:]
    carry_ref[c, 1] = uu[tm - SUBLANES:]
    act = (_silu(yg) * yu).astype(BF16)
    acc_ref[...] += _dot(act, wd_ref[...])

    @pl.when(c == nc - 1)
    def _():
        o_ref[...] = x_ref[...] + _rms(acc_ref[...], gpost_ref[...])


def _ffn(x, B, T, g_pre, w_up, conv_w, conv_b, w_down, g_post, *, tm, tf):
    M, D = x.shape
    F = w_down.shape[0]
    assert T % tm == 0 and F % tf == 0 and tm % SUBLANES == 0
    nt, nc = T // tm, F // tf
    conv_b = conv_b.reshape(1, 2 * F)
    limit = _vmem_limit(4 * _nbytes((tm, D), F32), _nbytes((tm, D), BF16), _nbytes((tm, D), F32),
                        4 * _nbytes((D, tf), BF16), 2 * _nbytes((tf, D), BF16),
                        8 * _nbytes((tm, tf), F32), _nbytes((tm, D), F32), 4 * 2**20)
    return pl.pallas_call(
        _ffn_kernel,
        grid=(B, nt, nc),
        in_specs=[pl.BlockSpec((tm, D), lambda b, t, c: (b * nt + t, 0)),
                  pl.BlockSpec((1, D), lambda b, t, c: (0, 0)),
                  pl.BlockSpec((D, tf), lambda b, t, c: (0, c)),
                  pl.BlockSpec((D, tf), lambda b, t, c: (0, nc + c)),
                  pl.BlockSpec((FFN_CONV_WIDTH, tf), lambda b, t, c: (0, c)),
                  pl.BlockSpec((FFN_CONV_WIDTH, tf), lambda b, t, c: (0, nc + c)),
                  pl.BlockSpec((1, tf), lambda b, t, c: (0, c)),
                  pl.BlockSpec((1, tf), lambda b, t, c: (0, nc + c)),
                  pl.BlockSpec((tf, D), lambda b, t, c: (c, 0)),
                  pl.BlockSpec((1, D), lambda b, t, c: (0, 0))],
        out_specs=pl.BlockSpec((tm, D), lambda b, t, c: (b * nt + t, 0)),
        out_shape=jax.ShapeDtypeStruct((M, D), F32),
        scratch_shapes=[pltpu.VMEM((tm, D), BF16),
                        pltpu.VMEM((tm, D), F32),
                        pltpu.VMEM((nc, 2, SUBLANES, tf), F32)],
        compiler_params=pltpu.CompilerParams(dimension_semantics=("parallel", "arbitrary", "arbitrary"),
                                             vmem_limit_bytes=limit),
        name="ffn",
    )(x, g_pre.reshape(1, D), w_up, w_up, conv_w, conv_w, conv_b, conv_b, w_down, g_post.reshape(1, D))


def _hgrn_sum_matrix(C):
    L = int(math.log2(C))
    t = np.arange(C)[:, None]
    j = np.arange(C)[None, :]
    blocks = [j <= t, j > t]
    for l in range(L):
        half, blk = 1 << l, 2 << l
        mid = t - t % blk + half - 1
        is_query = (t % blk) >= half
        blocks.append(np.where(is_query, (j > mid) & (j <= t), (j > t) & (j <= mid)))
    return np.concatenate(blocks, axis=0).astype(np.float32)


def _hgrn_kernel(q_ref, f_ref, i_ref, g_ref, lbp_ref, ng_ref, sm_ref, o_ref, st_ref, *, layer, C, NH):
    tc = q_ref.shape[0]
    L = int(math.log2(C))
    hd = HGRN_HEAD_DIM

    @pl.when(pl.program_id(2) == 0)
    def _():
        st_ref[...] = jnp.zeros(st_ref.shape, F32)

    p = lbp_ref[...]
    e = jnp.exp(p - jnp.max(p, axis=0, keepdims=True))
    lb_all = jnp.sum(e[:layer + 1], axis=0, keepdims=True) / jnp.sum(e, axis=0, keepdims=True)
    ng = ng_ref[...]
    sm = sm_ref[...]

    row = lax.broadcasted_iota(jnp.int32, (C, C), 0)
    col = lax.broadcasted_iota(jnp.int32, (C, C), 1)
    level_mask = [((row >> (l + 1)) == (col >> (l + 1))) & (((row >> l) & 1) == 1) & (((col >> l) & 1) == 0)
                  for l in range(L)]

    def chunk(ci, carry):
        r0 = pl.multiple_of(ci * C, C)
        for hh in range(NH):
            cols = slice(hh * hd, (hh + 1) * hd)
            lb = lb_all[:, cols]
            q = _silu(q_ref[pl.ds(r0, C), cols])
            f = lb + (1.0 - lb) * _sigmoid(f_ref[pl.ds(r0, C), cols])
            k = 1.0 - f
            v = i_ref[pl.ds(r0, C), cols].astype(BF16)
            z = _dot_01_f32(sm, jnp.log(f), pieces=2)
            b = z[:C]
            sfx = z[C:2 * C]
            st = st_ref[hh]
            inter = _dot_nt((q * jnp.exp(b)).astype(BF16), st.astype(BF16))
            a = jnp.where(row == col, jnp.sum(q * k, axis=-1, keepdims=True), 0.0)
            for l in range(L):
                el = jnp.exp(z[(2 + l) * C:(3 + l) * C])
                al = _dot_nt((q * el).astype(BF16), (k * el).astype(BF16))
                a = a + jnp.where(level_mask[l], al, 0.0)
            o = inter + _dot(a.astype(BF16), v)
            st_ref[hh] = jnp.exp(b[C - 1:C]) * st + _dot_tn(v, (k * jnp.exp(sfx)).astype(BF16))
            o = _rms(o, ng) * _silu(g_ref[pl.ds(r0, C), cols])
            o_ref[pl.ds(r0, C), cols] = o.astype(o_ref.dtype)
        return carry

    lax.fori_loop(0, tc // C, chunk, 0, unroll=2)


def _hgrn(proj, B, T, lb_param, norm_g, *, layer, tc, heads_per_step):
    M, W4 = proj.shape
    W = W4 // 4
    H = W // HGRN_HEAD_DIM
    C = HGRN_CHUNK
    NH = heads_per_step
    assert T % tc == 0 and tc % C == 0 and H % NH == 0
    nt = T // tc
    HG = H // NH
    sm = jnp.asarray(_hgrn_sum_matrix(C), BF16)
    hd = HGRN_HEAD_DIM
    bw = NH * hd
    blk = lambda off: pl.BlockSpec((tc, bw), lambda b, h, t, off=off: (b * nt + t, off * HG + h))
    return pl.pallas_call(
        functools.partial(_hgrn_kernel, layer=layer, C=C, NH=NH),
        grid=(B, HG, nt),
        in_specs=[blk(0), blk(1), blk(2), blk(3),
                  pl.BlockSpec((lb_param.shape[0], bw), lambda b, h, t: (0, h)),
                  pl.BlockSpec((1, hd), lambda b, h, t: (0, 0)),
                  pl.BlockSpec(sm.shape, lambda b, h, t: (0, 0))],
        out_specs=pl.BlockSpec((tc, bw), lambda b, h, t: (b * nt + t, h)),
        out_shape=jax.ShapeDtypeStruct((M, W), BF16),
        scratch_shapes=[pltpu.VMEM((NH, hd, hd), F32)],
        compiler_params=pltpu.CompilerParams(dimension_semantics=("parallel", "parallel", "arbitrary")),
        name="hgrn2",
    )(proj, proj, proj, proj, lb_param, norm_g.reshape(1, hd), sm)


def _rope_table_kernel(pos_ref, cs_ref, sup_ref, sdn_ref):
    half = ROT_DIM // 2
    lane = lax.broadcasted_iota(jnp.int32, (1, LANES), 1)
    d = lane % SWA_HEAD_DIM
    inv_freq = jnp.exp((d % half).astype(F32) * (-math.log(ROPE_THETA) / half))
    ang = pos_ref[...] * inv_freq
    cos, sin = jnp.cos(ang), jnp.sin(ang)
    cs_ref[...] = jnp.where(d < ROT_DIM, cos, 1.0)
    sup_ref[...] = jnp.where(d < half, -sin, 0.0)
    sdn_ref[...] = jnp.where((d >= half) & (d < ROT_DIM), sin, 0.0)


def _rope_tables(positions):
    T = positions.shape[0]
    out = jax.ShapeDtypeStruct((T, LANES), F32)
    return pl.pallas_call(
        _rope_table_kernel,
        grid=(1,),
        in_specs=[pl.BlockSpec((T, 1), lambda i: (0, 0))],
        out_specs=[pl.BlockSpec((T, LANES), lambda i: (0, 0))] * 3,
        out_shape=[out, out, out],
        name="rope_tables",
    )(positions.astype(F32).reshape(T, 1))


def _rope(x, cs, sup, sdn):
    w = x.shape[1]
    rep = w // LANES
    tile = lambda a: a if rep == 1 else jnp.concatenate([a] * rep, axis=1)
    half = ROT_DIM // 2
    return x * tile(cs) + pltpu.roll(x, w - half, 1) * tile(sup) + pltpu.roll(x, half, 1) * tile(sdn)


def _swa_kernel(sink_ref, q_ref, kc_ref, kp_ref, vc_ref, vp_ref,
                csc_ref, supc_ref, sdnc_ref, csp_ref, supp_ref, sdnp_ref, o_ref, *, n_kv):
    n = pl.program_id(1)
    W = SWA_WINDOW
    G = SWA_GROUP
    lane = lax.broadcasted_iota(jnp.int32, (1, LANES), 1)
    lo_half = lane < SWA_HEAD_DIM

    q = _rope(q_ref[...], csc_ref[...], supc_ref[...], sdnc_ref[...]) * (SWA_HEAD_DIM ** -0.5)
    k = jnp.concatenate([_rope(kp_ref[...], csp_ref[...], supp_ref[...], sdnp_ref[...]),
                         _rope(kc_ref[...], csc_ref[...], supc_ref[...], sdnc_ref[...])], axis=0)
    v = jnp.concatenate([vp_ref[...], vc_ref[...]], axis=0)

    qi = lax.broadcasted_iota(jnp.int32, (W, 2 * W), 0)
    kj = lax.broadcasted_iota(jnp.int32, (W, 2 * W), 1)
    diff = qi + W - kj
    allowed = (diff >= 0) & (diff < W) & ((kj >= W) | (n > 0))
    allowed = jnp.concatenate([allowed] * (G // 2), axis=0)
    half_mask = [lo_half, jnp.logical_not(lo_half)]

    for j in range(n_kv):
        pair = slice((j // 2) * LANES, (j // 2 + 1) * LANES)
        kpair, vpair = k[:, pair], v[:, pair]
        k_half = [kpair, pltpu.roll(kpair, SWA_HEAD_DIM, 1)]
        v_half = [vpair, pltpu.roll(vpair, SWA_HEAD_DIM, 1)]
        if j % 2 == 1:
            k_half.reverse()
            v_half.reverse()
        outs = []
        for hh in range(2):
            heads = [j * G + g for g in range(hh, G, 2)]
            qs = jnp.concatenate(
                [jnp.where(half_mask[hh], q[:, (h // 2) * LANES:(h // 2 + 1) * LANES], 0.0) for h in heads],
                axis=0).astype(BF16)
            s = _dot_nt(qs, k_half[hh].astype(BF16))
            s = jnp.where(allowed, s, -jnp.inf)
            sink = jnp.concatenate([jnp.full((W, 1), sink_ref[h], F32) for h in heads], axis=0)
            m = jnp.maximum(jnp.max(s, axis=-1, keepdims=True), sink)
            e = jnp.exp(s - m)
            p = e / (jnp.sum(e, axis=-1, keepdims=True) + jnp.exp(sink - m))
            outs.append(_dot(p.astype(BF16), v_half[hh].astype(BF16)))
        for gp in range(G // 2):
            pr = j * (G // 2) + gp
            rows = slice(gp * W, (gp + 1) * W)
            o_ref[:, pr * LANES:(pr + 1) * LANES] = jnp.where(lo_half, outs[0][rows], outs[1][rows]).astype(o_ref.dtype)


def _swa(proj, B, T, sinks, tables, *, n_q, n_kv):
    M = proj.shape[0]
    W = SWA_WINDOW
    assert T % W == 0 and n_q == n_kv * SWA_GROUP and n_kv % 2 == 0
    nblk = T // W
    qw, kw = n_q * SWA_HEAD_DIM, n_kv * SWA_HEAD_DIM
    assert qw % kw == 0
    kcol = qw // kw
    cs, sup, sdn = tables
    cur = lambda b, n: (b * nblk + n, 0)
    tcur = lambda b, n: (n, 0)
    tprev = lambda b, n: (jnp.maximum(n - 1, 0), 0)
    tspec = lambda im: pl.BlockSpec((W, LANES), im)
    return pl.pallas_call(
        functools.partial(_swa_kernel, n_kv=n_kv),
        grid=(B, nblk),
        in_specs=[pl.BlockSpec(memory_space=pltpu.SMEM),
                  pl.BlockSpec((W, qw), cur),
                  pl.BlockSpec((W, kw), lambda b, n: (b * nblk + n, kcol)),
                  pl.BlockSpec((W, kw), lambda b, n: (b * nblk + jnp.maximum(n - 1, 0), kcol)),
                  pl.BlockSpec((W, kw), lambda b, n: (b * nblk + n, kcol + 1)),
                  pl.BlockSpec((W, kw), lambda b, n: (b * nblk + jnp.maximum(n - 1, 0), kcol + 1)),
                  tspec(tcur), tspec(tcur), tspec(tcur), tspec(tprev), tspec(tprev), tspec(tprev)],
        out_specs=pl.BlockSpec((W, qw), cur),
        out_shape=jax.ShapeDtypeStruct((M, qw), BF16),
        compiler_params=pltpu.CompilerParams(dimension_semantics=("parallel", "parallel")),
        name="swa",
    )(sinks, proj, proj, proj, proj, proj, cs, sup, sdn, cs, sup, sdn)


def _sconv_kernel(b_ref, c_ref, x_ref, w_ref, o_ref, carry_ref):
    tt = b_ref.shape[0]

    @pl.when(pl.program_id(1) == 0)
    def _():
        carry_ref[...] = jnp.zeros(carry_ref.shape, F32)

    u = c_ref[...] * x_ref[...]
    y = _causal_conv3(u, carry_ref[...], w_ref[...])
    carry_ref[...] = u[tt - SUBLANES:]
    o_ref[...] = (b_ref[...] * y).astype(o_ref.dtype)


def _sconv(proj, B, T, conv_w, *, tt):
    M, D3 = proj.shape
    D = D3 // 3
    assert T % tt == 0
    nt = T // tt
    blk = lambda off: pl.BlockSpec((tt, D), lambda b, t, off=off: (b * nt + t, off))
    return pl.pallas_call(
        _sconv_kernel,
        grid=(B, nt),
        in_specs=[blk(0), blk(1), blk(2), pl.BlockSpec(conv_w.shape, lambda b, t: (0, 0))],
        out_specs=pl.BlockSpec((tt, D), lambda b, t: (b * nt + t, 0)),
        out_shape=jax.ShapeDtypeStruct((M, D), BF16),
        scratch_shapes=[pltpu.VMEM((SUBLANES, D), F32)],
        compiler_params=pltpu.CompilerParams(dimension_semantics=("parallel", "arbitrary"),
                                             vmem_limit_bytes=_vmem_limit(12 * _nbytes((tt, D), F32), 4 * 2**20)),
        name="sconv",
    )(proj, proj, proj, conv_w)


def _fox_cum_kernel(f_ref, bf_ref, tri_ref, c_ref, carry_ref):
    tb = f_ref.shape[0]

    @pl.when(pl.program_id(1) == 0)
    def _():
        carry_ref[...] = jnp.zeros(carry_ref.shape, F32)

    x = f_ref[...] + bf_ref[...]
    log_f = jnp.minimum(x, 0.0) - jnp.log(1.0 + jnp.exp(-jnp.abs(x)))
    c = _dot_01_f32(tri_ref[...], log_f) + carry_ref[...]
    carry_ref[...] = c[tb - 1:tb]
    c_ref[...] = c * LOG2E


def _fox_cum(f_logit, B, T, b_f, *, tb):
    assert T % tb == 0
    nt = T // tb
    tri = jnp.asarray(np.tril(np.ones((tb, tb), np.float32)), BF16)
    return pl.pallas_call(
        _fox_cum_kernel,
        grid=(B, nt),
        in_specs=[pl.BlockSpec((tb, LANES), lambda b, t: (b * nt + t, 0)),
                  pl.BlockSpec((1, LANES), lambda b, t: (0, 0)),
                  pl.BlockSpec((tb, tb), lambda b, t: (0, 0))],
        out_specs=pl.BlockSpec((tb, LANES), lambda b, t: (b * nt + t, 0)),
        out_shape=jax.ShapeDtypeStruct((B * T, LANES), F32),
        scratch_shapes=[pltpu.VMEM((1, LANES), F32)],
        compiler_params=pltpu.CompilerParams(dimension_semantics=("parallel", "arbitrary")),
        name="fox_cum",
    )(f_logit, b_f.reshape(1, LANES), tri)


def _fox_kernel(q_ref, k_ref, v_ref, g_ref, c_ref, o_ref, kx_ref, vt_ref, acc_ref, m_ref, l_ref, s_ref, mb_ref,
                *, tq, tk):
    h = pl.program_id(1)
    qi = pl.program_id(2)
    nk = vt_ref.shape[0]
    hd = FOX_HEAD_DIM

    @pl.when(qi == 0)
    def _():
        r = lax.broadcasted_iota(jnp.int32, (LANES, LANES), 0)
        cc = lax.broadcasted_iota(jnp.int32, (LANES, LANES), 1)

        def sel(piece):
            hit = ((cc == piece) & (r == 2 * h)) | ((cc == piece + 3) & (r == 2 * h + 1))
            return jnp.where(hit, 1.0, 0.0).astype(BF16)

        s_hi, s_mid, s_lo = sel(0), sel(1), sel(2)
        for j in range(nk):
            rows = slice(j * tk, (j + 1) * tk)
            kx_ref[rows, :LANES] = k_ref[rows, :].astype(BF16)
            c = c_ref[rows, :]
            hi = c.astype(BF16)
            r1 = c - hi.astype(F32)
            mid = r1.astype(BF16)
            lo = (r1 - mid.astype(F32)).astype(BF16)
            kx_ref[rows, LANES:] = (_dot(hi, s_hi) + _dot(mid, s_mid) + _dot(lo, s_lo)).astype(BF16)
            vt_ref[j] = v_ref[rows, :].T.astype(BF16)

    lane = lax.broadcasted_iota(jnp.int32, (1, LANES), 1)
    lo_half = lane < hd
    q2 = q_ref[...] * (hd ** -0.5 * LOG2E)
    qx = []
    for hh in range(2):
        qm = jnp.where(lo_half if hh == 0 else jnp.logical_not(lo_half), q2, 0.0)
        minus_one = jnp.where((lane >= 3 * hh) & (lane < 3 * hh + 3), -1.0, 0.0)
        qx.append(jnp.concatenate([qm, jnp.broadcast_to(minus_one, (tq, LANES))], axis=1).astype(BF16))

    m_ref[...] = jnp.full(m_ref.shape, -jnp.inf, F32)
    l_ref[...] = jnp.zeros(l_ref.shape, F32)
    acc_ref[...] = jnp.zeros(acc_ref.shape, F32)
    key_i = lax.broadcasted_iota(jnp.int32, (tk, tq), 0)
    qry_i = lax.broadcasted_iota(jnp.int32, (tk, tq), 1)

    def produce(j, buf, diag):
        kxb = kx_ref[pl.ds(pl.multiple_of(j * tk, tk), tk), :]
        for hh in range(2):
            st = _dot_nt(kxb, qx[hh])
            if diag is not None:
                st = jnp.where(key_i + diag * tk <= qry_i, st, -jnp.inf)
            s_ref[buf, hh] = st
            mb_ref[buf, hh:hh + 1, :] = jnp.max(st, axis=0, keepdims=True)

    def consume(j, buf):
        vtb = vt_ref[j]
        for hh in range(2):
            m_old = m_ref[hh:hh + 1, :]
            m_new = jnp.maximum(m_old, mb_ref[buf, hh:hh + 1, :])
            alpha = jnp.exp2(m_old - m_new)
            p = jnp.exp2(s_ref[buf, hh] - m_new)
            l_ref[hh:hh + 1, :] = alpha * l_ref[hh:hh + 1, :] + jnp.sum(p, axis=0, keepdims=True)
            rows = slice(hh * hd, (hh + 1) * hd)
            acc_ref[rows, :] = alpha * acc_ref[rows, :] + _dot(vtb[rows, :], p.astype(BF16))
            m_ref[hh:hh + 1, :] = m_new

    assert tq == 2 * tk

    def pair(i, next_is_diagonal):
        produce(2 * i + 1, 1, None)
        consume(2 * i, 0)
        produce(2 * i + 2, 0, 0 if next_is_diagonal else None)
        consume(2 * i + 1, 1)

    def full_pair(i, carry):
        pair(i, False)
        return carry

    @pl.when(qi == 0)
    def _():
        produce(0, 0, 0)

    @pl.when(qi > 0)
    def _():
        produce(0, 0, None)
        lax.fori_loop(0, qi - 1, full_pair, 0)
        pair(qi - 1, True)

    produce(2 * qi + 1, 1, 1)
    consume(2 * qi, 0)
    consume(2 * qi + 1, 1)
    ot = jnp.concatenate([acc_ref[:hd, :] / l_ref[0:1, :], acc_ref[hd:, :] / l_ref[1:2, :]], axis=0)
    o_ref[...] = (ot.T * _sigmoid(g_ref[...])).astype(o_ref.dtype)


def _fox_attn(qkvg, B, T, c2, *, n_heads, tq, tk):
    M, W4 = qkvg.shape
    W = W4 // 4
    assert W == n_heads * FOX_HEAD_DIM and T % tq == 0 and tq % tk == 0 and n_heads % 2 == 0
    hp = n_heads // 2
    nq, nk = T // tq, T // tk
    limit = _vmem_limit(6 * _nbytes((T, LANES), F32), _nbytes((T, 2 * LANES), BF16), _nbytes((T, LANES), BF16),
                        16 * _nbytes((tk, tq), F32), 8 * 2**20)
    return pl.pallas_call(
        functools.partial(_fox_kernel, tq=tq, tk=tk),
        grid=(B, hp, nq),
        in_specs=[pl.BlockSpec((tq, LANES), lambda b, h, i: (b * nq + i, h)),
                  pl.BlockSpec((T, LANES), lambda b, h, i: (b, hp + h)),
                  pl.BlockSpec((T, LANES), lambda b, h, i: (b, 2 * hp + h)),
                  pl.BlockSpec((tq, LANES), lambda b, h, i: (b * nq + i, 3 * hp + h)),
                  pl.BlockSpec((T, LANES), lambda b, h, i: (b, 0))],
        out_specs=pl.BlockSpec((tq, LANES), lambda b, h, i: (b * nq + i, h)),
        out_shape=jax.ShapeDtypeStruct((M, W), BF16),
        scratch_shapes=[pltpu.VMEM((T, 2 * LANES), BF16), pltpu.VMEM((nk, LANES, tk), BF16),
                        pltpu.VMEM((LANES, tq), F32), pltpu.VMEM((2, tq), F32), pltpu.VMEM((2, tq), F32),
                        pltpu.VMEM((2, 2, tk, tq), F32), pltpu.VMEM((2, 2, tq), F32)],
        compiler_params=pltpu.CompilerParams(dimension_semantics=("parallel", "parallel", "arbitrary"),
                                             vmem_limit_bytes=limit),
        name="fox_attn",
    )(qkvg, qkvg, qkvg, qkvg, c2)


def _pick(n, prefs):
    for t in prefs:
        if n % t == 0:
            return t
    raise ValueError(f"no tile in {prefs} divides {n}")


def kernel(x, positions, mix_pre_g, mix_post_g, ffn_pre_g, ffn_post_g, hgrn_w_in, hgrn_w_out, hgrn_norm_g, hgrn_lb_param, swa_w_in, swa_w_out, swa_sinks, sc_w_in, sc_conv_w, sc_w_out, fox_w_in, fox_b_f, fox_w_out, ffn_w_up, ffn_conv_w, ffn_conv_b, ffn_w_down):
    B, T, D = x.shape
    M = B * T
    depth = mix_pre_g.shape[0]
    xf = x.reshape(M, D)
    tm_proj = _pick(M, (1024, 512, 256, 128))
    tm_out = _pick(M, (512, 256, 128))
    tm_ffn = _pick(T, (512, 256, 128))
    bf = lambda w: w.astype(BF16)
    tables = None

    for i in range(depth):
        m, j = i % N_MIXERS, i // N_MIXERS
        if m == 0:
            proj = _norm_matmul(xf, mix_pre_g[i], bf(hgrn_w_in[j]), tm=tm_proj, tn=None)
            a = _hgrn(proj, B, T, hgrn_lb_param, hgrn_norm_g[j], layer=i, tc=_pick(T, (512, 256, 128, 64)),
                      heads_per_step=4)
            w_out = hgrn_w_out[j]
        elif m == 1:
            n_q = swa_sinks.shape[1]
            n_kv = n_q // SWA_GROUP
            if tables is None:
                tables = _rope_tables(positions)
            proj = _norm_matmul(xf, mix_pre_g[i], bf(swa_w_in[j]), tm=tm_proj, tn=None)
            a = _swa(proj, B, T, swa_sinks[j], tables, n_q=n_q, n_kv=n_kv)
            w_out = swa_w_out[j]
        elif m == 2:
            proj = _norm_matmul(xf, mix_pre_g[i], bf(sc_w_in[j]), tm=tm_proj, tn=None)
            a = _sconv(proj, B, T, sc_conv_w[j], tt=_pick(T, (256, 128)))
            w_out = sc_w_out[j]
        else:
            n_heads = fox_b_f.shape[1]
            W = n_heads * FOX_HEAD_DIM
            w_in = fox_w_in[j]
            w_qkvg = bf(jnp.concatenate([w_in[:, :3 * W], w_in[:, 3 * W + n_heads:]], axis=1))
            w_f = bf(jnp.pad(w_in[:, 3 * W:3 * W + n_heads], ((0, 0), (0, LANES - n_heads))))
            b_f = jnp.pad(fox_b_f[j], (0, LANES - n_heads))
            qkvg = _norm_matmul(xf, mix_pre_g[i], w_qkvg, tm=tm_proj, tn=None)
            f_logit = _norm_matmul(xf, mix_pre_g[i], w_f, tm=tm_proj, tn=LANES)
            c2 = _fox_cum(f_logit, B, T, b_f, tb=_pick(T, (256, 128)))
            a = _fox_attn(qkvg, B, T, c2, n_heads=n_heads, tq=512, tk=256)
            w_out = fox_w_out[j]
        xf = _out_proj(a, bf(w_out), mix_post_g[i], xf, tm=tm_out)
        xf = _ffn(xf, B, T, ffn_pre_g[i], bf(ffn_w_up[i]), ffn_conv_w[i], ffn_conv_b[i], bf(ffn_w_down[i]),
                  ffn_post_g[i], tm=tm_ffn, tf=_pick(ffn_w_down.shape[1], (512, 256, 128)))
    return xf.reshape(B, T, D)
```

```python
import functools
import math

import numpy as np
import jax
import jax.numpy as jnp
from jax import lax
from jax.experimental import pallas as pl
from jax.experimental.pallas import tpu as pltpu

F32 = jnp.float32
BF16 = jnp.bfloat16

RMS_EPS = 1e-6
N_MIXERS = 4
HGRN_HEAD_DIM = 128
HGRN_CHUNK = 128
SWA_HEAD_DIM = 64
SWA_GROUP = 8
SWA_WINDOW = 128
ROPE_THETA = 500000.0
ROT_DIM = 16
FOX_HEAD_DIM = 64
FFN_CONV_WIDTH = 3
FFN_GATE_ROWS = 128
LOG2E = math.log2(math.e)

LANES = 128
SUBLANES = 8
V7X_VMEM_BYTES = 64 * 2**20
VMEM_CAP_BYTES = 56 * 2**20


def _vmem_limit(*nbytes):
    return int(min(VMEM_CAP_BYTES, sum(nbytes)))


def _nbytes(shape, dtype):
    return int(np.prod(shape)) * jnp.dtype(dtype).itemsize


def _rms(x, g):
    ms = jnp.mean(x * x, axis=-1, keepdims=True)
    return x * lax.rsqrt(ms + RMS_EPS) * g


def _dot(a, b):
    return jnp.dot(a, b, preferred_element_type=F32)


def _dot_nt(a, b):
    return lax.dot_general(a, b, (((1,), (1,)), ((), ())), preferred_element_type=F32)


def _dot_tn(a, b):
    return lax.dot_general(a, b, (((0,), (0,)), ((), ())), preferred_element_type=F32)


def _dot_01_f32(m01, x, pieces=3):
    n = x.shape[1]
    terms, rest = [], x
    for _ in range(pieces):
        t = rest.astype(BF16)
        terms.append(t)
        rest = rest - t.astype(F32)
    y = _dot(m01, jnp.concatenate(terms, axis=1))
    out = y[:, :n]
    for i in range(1, pieces):
        out = out + y[:, i * n:(i + 1) * n]
    return out


def _sigmoid(x):
    return 1.0 / (1.0 + jnp.exp(-x))


def _silu(x):
    return x * _sigmoid(x)


def _causal_conv3(u, prev8, w):
    w0, w1, w2 = w[0:1], w[1:2], w[2:3]
    r1 = pltpu.roll(u, 1, 0)
    r2 = pltpu.roll(u, 2, 0)
    y = w2 * u + w1 * r1 + w0 * r2
    rows = lax.broadcasted_iota(jnp.int32, prev8.shape, 0)
    h1 = jnp.where(rows < 1, pltpu.roll(prev8, 1, 0), r1[:SUBLANES])
    h2 = jnp.where(rows < 2, pltpu.roll(prev8, 2, 0), r2[:SUBLANES])
    yh = w2 * u[:SUBLANES] + w1 * h1 + w0 * h2
    return jnp.concatenate([yh, y[SUBLANES:]], axis=0)


def _norm_matmul_kernel(x_ref, g_ref, w_ref, o_ref, hn_ref):
    @pl.when(pl.program_id(1) == 0)
    def _():
        hn_ref[...] = _rms(x_ref[...], g_ref[...]).astype(BF16)

    o_ref[...] = _dot(hn_ref[...], w_ref[...]).astype(o_ref.dtype)


def _norm_matmul(x, g, w, *, tm, tn, out_dtype=F32):
    M, D = x.shape
    N = w.shape[1]
    if tn is None:
        tn = next(t for t in (1024, 1280, 768, 512, 256, LANES) if N % t == 0)
    assert M % tm == 0 and N % tn == 0
    limit = _vmem_limit(2 * _nbytes((tm, D), F32), _nbytes((tm, D), BF16), 2 * _nbytes((D, tn), BF16),
                        3 * _nbytes((tm, tn), F32), _nbytes((tm, D), F32), 4 * 2**20)
    return pl.pallas_call(
        _norm_matmul_kernel,
        grid=(M // tm, N // tn),
        in_specs=[pl.BlockSpec((tm, D), lambda i, j: (i, 0)),
                  pl.BlockSpec((1, D), lambda i, j: (0, 0)),
                  pl.BlockSpec((D, tn), lambda i, j: (0, j))],
        out_specs=pl.BlockSpec((tm, tn), lambda i, j: (i, j)),
        out_shape=jax.ShapeDtypeStruct((M, N), out_dtype),
        scratch_shapes=[pltpu.VMEM((tm, D), BF16)],
        compiler_params=pltpu.CompilerParams(dimension_semantics=("parallel", "arbitrary"),
                                             vmem_limit_bytes=limit),
        name="norm_matmul",
    )(x, g.reshape(1, D), w)


def _out_proj_kernel(a_ref, w_ref, g_ref, x_ref, o_ref):
    y = _dot(a_ref[...], w_ref[...])
    o_ref[...] = x_ref[...] + _rms(y, g_ref[...])


def _out_proj(a, w, g, x, *, tm):
    M, K = a.shape
    D = w.shape[1]
    assert M % tm == 0
    limit = _vmem_limit(2 * _nbytes((tm, K), BF16), 2 * _nbytes((K, D), BF16), 4 * _nbytes((tm, D), F32),
                        2 * _nbytes((tm, D), F32), 4 * 2**20)
    return pl.pallas_call(
        _out_proj_kernel,
        grid=(M // tm,),
        in_specs=[pl.BlockSpec((tm, K), lambda i: (i, 0)),
                  pl.BlockSpec((K, D), lambda i: (0, 0)),
                  pl.BlockSpec((1, D), lambda i: (0, 0)),
                  pl.BlockSpec((tm, D), lambda i: (i, 0))],
        out_specs=pl.BlockSpec((tm, D), lambda i: (i, 0)),
        out_shape=jax.ShapeDtypeStruct((M, D), F32),
        compiler_params=pltpu.CompilerParams(dimension_semantics=("parallel",), vmem_limit_bytes=limit),
        name="out_proj",
    )(a, w, g.reshape(1, D), x)


def _ffn_kernel(x_ref, gpre_ref, wg_ref, wu_ref, cwg_ref, cwu_ref, cbg_ref, cbu_ref, wd_ref, gpost_ref,
                o_ref, hn_ref, acc_ref, carry_ref, act_ref):
    ti = pl.program_id(1)
    c = pl.program_id(2)
    nc = pl.num_programs(2)
    tm = x_ref.shape[0]
    tf = wg_ref.shape[1]
    R = FFN_GATE_ROWS

    @pl.when(c == 0)
    def _():
        hn_ref[...] = _rms(x_ref[...], gpre_ref[...]).astype(BF16)
        acc_ref[...] = jnp.zeros(acc_ref.shape, F32)

    @pl.when(ti == 0)
    def _():
        carry_ref[c] = jnp.zeros(carry_ref.shape[1:], F32)

    hn = hn_ref[...]
    u = (_dot(hn, wg_ref[...]), _dot(hn, wu_ref[...]))
    prev = (carry_ref[c, 0], carry_ref[c, 1])
    carry_ref[c, 0] = u[0][tm - SUBLANES:]
    carry_ref[c, 1] = u[1][tm - SUBLANES:]
    cw = (cwg_ref, cwu_ref)
    cb = (cbg_ref, cbu_ref)

    def conv(k, r, cols):
        head = prev[k][:, cols] if r == 0 else u[k][r * R - SUBLANES:r * R, cols]
        blk = jnp.concatenate([head, u[k][r * R:(r + 1) * R, cols]], axis=0)
        w = cw[k][:, cols]
        y = w[2:3] * blk + w[1:2] * pltpu.roll(blk, 1, 0) + w[0:1] * pltpu.roll(blk, 2, 0)
        return y[SUBLANES:] + cb[k][:, cols]

    for j in range(tf // LANES):
        cols = slice(j * LANES, (j + 1) * LANES)
        for r in range(tm // R):
            act_ref[r * R:(r + 1) * R, cols] = (_silu(conv(0, r, cols)) * conv(1, r, cols)).astype(BF16)
    acc_ref[...] += _dot(act_ref[...], wd_ref[...])

    @pl.when(c == nc - 1)
    def _():
        o_ref[...] = x_ref[...] + _rms(acc_ref[...], gpost_ref[...])


def _ffn(x, B, T, g_pre, w_up, conv_w, conv_b, w_down, g_post, *, tm, tf):
    M, D = x.shape
    F = w_down.shape[0]
    assert T % tm == 0 and F % tf == 0 and tm % FFN_GATE_ROWS == 0 and tf % LANES == 0
    nt, nc = T // tm, F // tf
    conv_b = conv_b.reshape(1, 2 * F)
    limit = _vmem_limit(4 * _nbytes((tm, D), F32), _nbytes((tm, D), BF16), _nbytes((tm, D), F32),
                        4 * _nbytes((D, tf), BF16), 2 * _nbytes((tf, D), BF16),
                        8 * _nbytes((tm, tf), F32), _nbytes((tm, D), F32), 4 * 2**20)
    return pl.pallas_call(
        _ffn_kernel,
        grid=(B, nt, nc),
        in_specs=[pl.BlockSpec((tm, D), lambda b, t, c: (b * nt + t, 0)),
                  pl.BlockSpec((1, D), lambda b, t, c: (0, 0)),
                  pl.BlockSpec((D, tf), lambda b, t, c: (0, c)),
                  pl.BlockSpec((D, tf), lambda b, t, c: (0, nc + c)),
                  pl.BlockSpec((FFN_CONV_WIDTH, tf), lambda b, t, c: (0, c)),
                  pl.BlockSpec((FFN_CONV_WIDTH, tf), lambda b, t, c: (0, nc + c)),
                  pl.BlockSpec((1, tf), lambda b, t, c: (0, c)),
                  pl.BlockSpec((1, tf), lambda b, t, c: (0, nc + c)),
                  pl.BlockSpec((tf, D), lambda b, t, c: (c, 0)),
                  pl.BlockSpec((1, D), lambda b, t, c: (0, 0))],
        out_specs=pl.BlockSpec((tm, D), lambda b, t, c: (b * nt + t, 0)),
        out_shape=jax.ShapeDtypeStruct((M, D), F32),
        scratch_shapes=[pltpu.VMEM((tm, D), BF16),
                        pltpu.VMEM((tm, D), F32),
                        pltpu.VMEM((nc, 2, SUBLANES, tf), F32),
                        pltpu.VMEM((tm, tf), BF16)],
        compiler_params=pltpu.CompilerParams(dimension_semantics=("parallel", "arbitrary", "arbitrary"),
                                             vmem_limit_bytes=limit),
        name="ffn",
    )(x, g_pre.reshape(1, D), w_up, w_up, conv_w, conv_w, conv_b, conv_b, w_down, g_post.reshape(1, D))


def _hgrn_sum_matrix(C):
    L = int(math.log2(C))
    t = np.arange(C)[:, None]
    j = np.arange(C)[None, :]
    blocks = [j <= t, j > t]
    for l in range(L):
        half, blk = 1 << l, 2 << l
        mid = t - t % blk + half - 1
        is_query = (t % blk) >= half
        blocks.append(np.where(is_query, (j > mid) & (j <= t), (j > t) & (j <= mid)))
    return np.concatenate(blocks, axis=0).astype(np.float32)


def _hgrn_kernel(q_ref, f_ref, i_ref, g_ref, lbp_ref, ng_ref, sm_ref, o_ref, st_ref, *, layer, C, NH):
    tc = q_ref.shape[0]
    L = int(math.log2(C))
    hd = HGRN_HEAD_DIM

    @pl.when(pl.program_id(2) == 0)
    def _():
        st_ref[...] = jnp.zeros(st_ref.shape, F32)

    p = lbp_ref[...]
    e = jnp.exp(p - jnp.max(p, axis=0, keepdims=True))
    lb_all = jnp.sum(e[:layer + 1], axis=0, keepdims=True) / jnp.sum(e, axis=0, keepdims=True)
    ng = ng_ref[...]
    sm = sm_ref[...]

    row = lax.broadcasted_iota(jnp.int32, (C, C), 0)
    col = lax.broadcasted_iota(jnp.int32, (C, C), 1)
    level_mask = [((row >> (l + 1)) == (col >> (l + 1))) & (((row >> l) & 1) == 1) & (((col >> l) & 1) == 0)
                  for l in range(L)]

    def chunk(ci, carry):
        r0 = pl.multiple_of(ci * C, C)
        for hh in range(NH):
            cols = slice(hh * hd, (hh + 1) * hd)
            lb = lb_all[:, cols]
            q = _silu(q_ref[pl.ds(r0, C), cols])
            f = lb + (1.0 - lb) * _sigmoid(f_ref[pl.ds(r0, C), cols])
            k = 1.0 - f
            v = i_ref[pl.ds(r0, C), cols].astype(BF16)
            z = _dot_01_f32(sm, jnp.log(f), pieces=2)
            b = z[:C]
            sfx = z[C:2 * C]
            st = st_ref[hh]
            inter = _dot_nt((q * jnp.exp(b)).astype(BF16), st.astype(BF16))
            a = jnp.where(row == col, jnp.sum(q * k, axis=-1, keepdims=True), 0.0)
            for l in range(L):
                el = jnp.exp(z[(2 + l) * C:(3 + l) * C])
                al = _dot_nt((q * el).astype(BF16), (k * el).astype(BF16))
                a = a + jnp.where(level_mask[l], al, 0.0)
            o = inter + _dot(a.astype(BF16), v)
            st_ref[hh] = jnp.exp(b[C - 1:C]) * st + _dot_tn(v, (k * jnp.exp(sfx)).astype(BF16))
            o = _rms(o, ng) * _silu(g_ref[pl.ds(r0, C), cols])
            o_ref[pl.ds(r0, C), cols] = o.astype(o_ref.dtype)
        return carry

    lax.fori_loop(0, tc // C, chunk, 0, unroll=2)


def _hgrn(proj, B, T, lb_param, norm_g, *, layer, tc, heads_per_step):
    M, W4 = proj.shape
    W = W4 // 4
    H = W // HGRN_HEAD_DIM
    C = HGRN_CHUNK
    NH = heads_per_step
    assert T % tc == 0 and tc % C == 0 and H % NH == 0
    nt = T // tc
    HG = H // NH
    sm = jnp.asarray(_hgrn_sum_matrix(C), BF16)
    hd = HGRN_HEAD_DIM
    bw = NH * hd
    blk = lambda off: pl.BlockSpec((tc, bw), lambda b, h, t, off=off: (b * nt + t, off * HG + h))
    return pl.pallas_call(
        functools.partial(_hgrn_kernel, layer=layer, C=C, NH=NH),
        grid=(B, HG, nt),
        in_specs=[blk(0), blk(1), blk(2), blk(3),
                  pl.BlockSpec((lb_param.shape[0], bw), lambda b, h, t: (0, h)),
                  pl.BlockSpec((1, hd), lambda b, h, t: (0, 0)),
                  pl.BlockSpec(sm.shape, lambda b, h, t: (0, 0))],
        out_specs=pl.BlockSpec((tc, bw), lambda b, h, t: (b * nt + t, h)),
        out_shape=jax.ShapeDtypeStruct((M, W), BF16),
        scratch_shapes=[pltpu.VMEM((NH, hd, hd), F32)],
        compiler_params=pltpu.CompilerParams(dimension_semantics=("parallel", "parallel", "arbitrary")),
        name="hgrn2",
    )(proj, proj, proj, proj, lb_param, norm_g.reshape(1, hd), sm)


def _rope_table_kernel(pos_ref, cs_ref, sup_ref, sdn_ref):
    half = ROT_DIM // 2
    lane = lax.broadcasted_iota(jnp.int32, (1, LANES), 1)
    d = lane % SWA_HEAD_DIM
    inv_freq = jnp.exp((d % half).astype(F32) * (-math.log(ROPE_THETA) / half))
    ang = pos_ref[...] * inv_freq
    cos, sin = jnp.cos(ang), jnp.sin(ang)
    cs_ref[...] = jnp.where(d < ROT_DIM, cos, 1.0)
    sup_ref[...] = jnp.where(d < half, -sin, 0.0)
    sdn_ref[...] = jnp.where((d >= half) & (d < ROT_DIM), sin, 0.0)


def _rope_tables(positions):
    T = positions.shape[0]
    out = jax.ShapeDtypeStruct((T, LANES), F32)
    return pl.pallas_call(
        _rope_table_kernel,
        grid=(1,),
        in_specs=[pl.BlockSpec((T, 1), lambda i: (0, 0))],
        out_specs=[pl.BlockSpec((T, LANES), lambda i: (0, 0))] * 3,
        out_shape=[out, out, out],
        name="rope_tables",
    )(positions.astype(F32).reshape(T, 1))


def _rope(x, cs, sup, sdn):
    w = x.shape[1]
    rep = w // LANES
    tile = lambda a: a if rep == 1 else jnp.concatenate([a] * rep, axis=1)
    half = ROT_DIM // 2
    return x * tile(cs) + pltpu.roll(x, w - half, 1) * tile(sup) + pltpu.roll(x, half, 1) * tile(sdn)


def _swa_kernel(sink_ref, q_ref, kc_ref, kp_ref, vc_ref, vp_ref,
                csc_ref, supc_ref, sdnc_ref, csp_ref, supp_ref, sdnp_ref, o_ref, *, n_kv):
    n = pl.program_id(1)
    W = SWA_WINDOW
    G = SWA_GROUP
    lane = lax.broadcasted_iota(jnp.int32, (1, LANES), 1)
    lo_half = lane < SWA_HEAD_DIM

    q = _rope(q_ref[...], csc_ref[...], supc_ref[...], sdnc_ref[...]) * (SWA_HEAD_DIM ** -0.5 * LOG2E)
    k = jnp.concatenate([_rope(kp_ref[...], csp_ref[...], supp_ref[...], sdnp_ref[...]),
                         _rope(kc_ref[...], csc_ref[...], supc_ref[...], sdnc_ref[...])], axis=0)
    v = jnp.concatenate([vp_ref[...], vc_ref[...]], axis=0)

    S = (G // 2) * W
    kj = lax.broadcasted_iota(jnp.int32, (2 * W, S), 0)
    qi = lax.broadcasted_iota(jnp.int32, (2 * W, S), 1) & (W - 1)
    diff = qi + W - kj
    allowed = (diff >= 0) & (diff < W) & ((kj >= W) | (n > 0))
    mask_bias = jnp.where(allowed, 0.0, -jnp.inf)
    half_mask = [lo_half, jnp.logical_not(lo_half)]
    hd = SWA_HEAD_DIM

    for jp in range(n_kv // 2):
        pair = slice(jp * LANES, (jp + 1) * LANES)
        kpair = k[:, pair]
        k_same = kpair.astype(BF16)
        k_swap = pltpu.roll(kpair, hd, 1).astype(BF16)
        vt = v[:, pair].T.astype(BF16)
        for jj in range(2):
            j = 2 * jp + jj
            outs_t = []
            for hh in range(2):
                heads = [j * G + g for g in range(hh, G, 2)]
                qs = jnp.concatenate(
                    [jnp.where(half_mask[hh], q[:, (h // 2) * LANES:(h // 2 + 1) * LANES], 0.0) for h in heads],
                    axis=0).astype(BF16)
                st = _dot_nt(k_same if hh == jj else k_swap, qs) + mask_bias
                sink = jnp.concatenate([jnp.full((1, W), sink_ref[h] * LOG2E, F32) for h in heads], axis=1)
                m = jnp.maximum(jnp.max(st, axis=0, keepdims=True), sink)
                e = jnp.exp2(st - m)
                denom = jnp.sum(e, axis=0, keepdims=True) + jnp.exp2(sink - m)
                outs_t.append(_dot(vt[jj * hd:(jj + 1) * hd, :], e.astype(BF16)) / denom)
            for gp in range(G // 2):
                pr = j * (G // 2) + gp
                cols = slice(gp * W, (gp + 1) * W)
                ot = jnp.concatenate([outs_t[0][:, cols], outs_t[1][:, cols]], axis=0)
                o_ref[:, pr * LANES:(pr + 1) * LANES] = ot.T.astype(o_ref.dtype)


def _swa(proj, B, T, sinks, tables, *, n_q, n_kv):
    M = proj.shape[0]
    W = SWA_WINDOW
    assert T % W == 0 and n_q == n_kv * SWA_GROUP and n_kv % 2 == 0
    nblk = T // W
    qw, kw = n_q * SWA_HEAD_DIM, n_kv * SWA_HEAD_DIM
    assert qw % kw == 0
    kcol = qw // kw
    cs, sup, sdn = tables
    cur = lambda b, n: (b * nblk + n, 0)
    tcur = lambda b, n: (n, 0)
    tprev = lambda b, n: (jnp.maximum(n - 1, 0), 0)
    tspec = lambda im: pl.BlockSpec((W, LANES), im)
    return pl.pallas_call(
        functools.partial(_swa_kernel, n_kv=n_kv),
        grid=(B, nblk),
        in_specs=[pl.BlockSpec(memory_space=pltpu.SMEM),
                  pl.BlockSpec((W, qw), cur),
                  pl.BlockSpec((W, kw), lambda b, n: (b * nblk + n, kcol)),
                  pl.BlockSpec((W, kw), lambda b, n: (b * nblk + jnp.maximum(n - 1, 0), kcol)),
                  pl.BlockSpec((W, kw), lambda b, n: (b * nblk + n, kcol + 1)),
                  pl.BlockSpec((W, kw), lambda b, n: (b * nblk + jnp.maximum(n - 1, 0), kcol + 1)),
                  tspec(tcur), tspec(tcur), tspec(tcur), tspec(tprev), tspec(tprev), tspec(tprev)],
        out_specs=pl.BlockSpec((W, qw), cur),
        out_shape=jax.ShapeDtypeStruct((M, qw), BF16),
        compiler_params=pltpu.CompilerParams(dimension_semantics=("parallel", "parallel")),
        name="swa",
    )(sinks, proj, proj, proj, proj, proj, cs, sup, sdn, cs, sup, sdn)


def _sconv_kernel(b_ref, c_ref, x_ref, w_ref, o_ref, carry_ref):
    tt = b_ref.shape[0]

    @pl.when(pl.program_id(1) == 0)
    def _():
        carry_ref[...] = jnp.zeros(carry_ref.shape, F32)

    u = c_ref[...] * x_ref[...]
    y = _causal_conv3(u, carry_ref[...], w_ref[...])
    carry_ref[...] = u[tt - SUBLANES:]
    o_ref[...] = (b_ref[...] * y).astype(o_ref.dtype)


def _sconv(proj, B, T, conv_w, *, tt):
    M, D3 = proj.shape
    D = D3 // 3
    assert T % tt == 0
    nt = T // tt
    blk = lambda off: pl.BlockSpec((tt, D), lambda b, t, off=off: (b * nt + t, off))
    return pl.pallas_call(
        _sconv_kernel,
        grid=(B, nt),
        in_specs=[blk(0), blk(1), blk(2), pl.BlockSpec(conv_w.shape, lambda b, t: (0, 0))],
        out_specs=pl.BlockSpec((tt, D), lambda b, t: (b * nt + t, 0)),
        out_shape=jax.ShapeDtypeStruct((M, D), BF16),
        scratch_shapes=[pltpu.VMEM((SUBLANES, D), F32)],
        compiler_params=pltpu.CompilerParams(dimension_semantics=("parallel", "arbitrary"),
                                             vmem_limit_bytes=_vmem_limit(12 * _nbytes((tt, D), F32), 4 * 2**20)),
        name="sconv",
    )(proj, proj, proj, conv_w)


def _fox_cum_kernel(f_ref, bf_ref, tri_ref, c_ref, carry_ref):
    tb = f_ref.shape[0]

    @pl.when(pl.program_id(1) == 0)
    def _():
        carry_ref[...] = jnp.zeros(carry_ref.shape, F32)

    x = f_ref[...] + bf_ref[...]
    log_f = jnp.minimum(x, 0.0) - jnp.log(1.0 + jnp.exp(-jnp.abs(x)))
    c = _dot_01_f32(tri_ref[...], log_f) + carry_ref[...]
    carry_ref[...] = c[tb - 1:tb]
    c_ref[...] = c * LOG2E


def _fox_cum(f_logit, B, T, b_f, *, tb):
    assert T % tb == 0
    nt = T // tb
    tri = jnp.asarray(np.tril(np.ones((tb, tb), np.float32)), BF16)
    return pl.pallas_call(
        _fox_cum_kernel,
        grid=(B, nt),
        in_specs=[pl.BlockSpec((tb, LANES), lambda b, t: (b * nt + t, 0)),
                  pl.BlockSpec((1, LANES), lambda b, t: (0, 0)),
                  pl.BlockSpec((tb, tb), lambda b, t: (0, 0))],
        out_specs=pl.BlockSpec((tb, LANES), lambda b, t: (b * nt + t, 0)),
        out_shape=jax.ShapeDtypeStruct((B * T, LANES), F32),
        scratch_shapes=[pltpu.VMEM((1, LANES), F32)],
        compiler_params=pltpu.CompilerParams(dimension_semantics=("parallel", "arbitrary")),
        name="fox_cum",
    )(f_logit, b_f.reshape(1, LANES), tri)


def _fox_kernel(q_ref, k_ref, v_ref, g_ref, c_ref, o_ref, kx_ref, vt_ref, acc_ref, m_ref, l_ref, s_ref, mb_ref,
                *, tq, tk):
    h = pl.program_id(1)
    qi = pl.program_id(2)
    nk = vt_ref.shape[0]
    hd = FOX_HEAD_DIM

    @pl.when(qi == 0)
    def _():
        r = lax.broadcasted_iota(jnp.int32, (LANES, LANES), 0)
        cc = lax.broadcasted_iota(jnp.int32, (LANES, LANES), 1)

        def sel(piece):
            hit = ((cc == piece) & (r == 2 * h)) | ((cc == piece + 3) & (r == 2 * h + 1))
            return jnp.where(hit, 1.0, 0.0).astype(BF16)

        s_hi, s_mid, s_lo = sel(0), sel(1), sel(2)
        for j in range(nk):
            rows = slice(j * tk, (j + 1) * tk)
            kx_ref[rows, :LANES] = k_ref[rows, :].astype(BF16)
            c = c_ref[rows, :]
            hi = c.astype(BF16)
            r1 = c - hi.astype(F32)
            mid = r1.astype(BF16)
            lo = (r1 - mid.astype(F32)).astype(BF16)
            kx_ref[rows, LANES:] = (_dot(hi, s_hi) + _dot(mid, s_mid) + _dot(lo, s_lo)).astype(BF16)
            vt_ref[j] = v_ref[rows, :].T.astype(BF16)

    lane = lax.broadcasted_iota(jnp.int32, (1, LANES), 1)
    lo_half = lane < hd
    q2 = q_ref[...] * (hd ** -0.5 * LOG2E)
    qx = []
    for hh in range(2):
        qm = jnp.where(lo_half if hh == 0 else jnp.logical_not(lo_half), q2, 0.0)
        minus_one = jnp.where((lane >= 3 * hh) & (lane < 3 * hh + 3), -1.0, 0.0)
        qx.append(jnp.concatenate([qm, jnp.broadcast_to(minus_one, (tq, LANES))], axis=1).astype(BF16))

    m_ref[...] = jnp.full(m_ref.shape, -jnp.inf, F32)
    l_ref[...] = jnp.zeros(l_ref.shape, F32)
    acc_ref[...] = jnp.zeros(acc_ref.shape, F32)
    key_i = lax.broadcasted_iota(jnp.int32, (tk, tq), 0)
    qry_i = lax.broadcasted_iota(jnp.int32, (tk, tq), 1)

    def produce(j, buf, diag):
        kxb = kx_ref[pl.ds(pl.multiple_of(j * tk, tk), tk), :]
        for hh in range(2):
            st = _dot_nt(kxb, qx[hh])
            if diag is not None:
                st = jnp.where(key_i + diag * tk <= qry_i, st, -jnp.inf)
            s_ref[buf, hh] = st
            mb_ref[buf, hh:hh + 1, :] = jnp.max(st, axis=0, keepdims=True)

    def consume(j, buf):
        vtb = vt_ref[j]
        for hh in range(2):
            m_old = m_ref[hh:hh + 1, :]
            m_new = jnp.maximum(m_old, mb_ref[buf, hh:hh + 1, :])
            alpha = jnp.exp2(m_old - m_new)
            p = jnp.exp2(s_ref[buf, hh] - m_new)
            l_ref[hh:hh + 1, :] = alpha * l_ref[hh:hh + 1, :] + jnp.sum(p, axis=0, keepdims=True)
            rows = slice(hh * hd, (hh + 1) * hd)
            acc_ref[rows, :] = alpha * acc_ref[rows, :] + _dot(vtb[rows, :], p.astype(BF16))
            m_ref[hh:hh + 1, :] = m_new

    assert tq == 2 * tk

    def pair(i, next_is_diagonal):
        produce(2 * i + 1, 1, None)
        consume(2 * i, 0)
        produce(2 * i + 2, 0, 0 if next_is_diagonal else None)
        consume(2 * i + 1, 1)

    def full_pair(i, carry):
        pair(i, False)
        return carry

    def finish():
        produce(2 * qi + 1, 1, 1)
        consume(2 * qi, 0)
        consume(2 * qi + 1, 1)
        ot = jnp.concatenate([acc_ref[:hd, :] / l_ref[0:1, :], acc_ref[hd:, :] / l_ref[1:2, :]], axis=0)
        o_ref[...] = (ot.T * _sigmoid(g_ref[...])).astype(o_ref.dtype)

    @pl.when(qi == 0)
    def _():
        produce(0, 0, 0)
        finish()

    @pl.when(qi > 0)
    def _():
        produce(0, 0, None)
        lax.fori_loop(0, qi - 1, full_pair, 0)
        pair(qi - 1, True)
        finish()


def _fox_attn(qkvg, B, T, c2, *, n_heads, tq, tk):
    M, W4 = qkvg.shape
    W = W4 // 4
    assert W == n_heads * FOX_HEAD_DIM and T % tq == 0 and tq % tk == 0 and n_heads % 2 == 0
    hp = n_heads // 2
    nq, nk = T // tq, T // tk
    limit = _vmem_limit(6 * _nbytes((T, LANES), F32), _nbytes((T, 2 * LANES), BF16), _nbytes((T, LANES), BF16),
                        16 * _nbytes((tk, tq), F32), 8 * 2**20)
    return pl.pallas_call(
        functools.partial(_fox_kernel, tq=tq, tk=tk),
        grid=(B, hp, nq),
        in_specs=[pl.BlockSpec((tq, LANES), lambda b, h, i: (b * nq + i, h)),
                  pl.BlockSpec((T, LANES), lambda b, h, i: (b, hp + h)),
                  pl.BlockSpec((T, LANES), lambda b, h, i: (b, 2 * hp + h)),
                  pl.BlockSpec((tq, LANES), lambda b, h, i: (b * nq + i, 3 * hp + h)),
                  pl.BlockSpec((T, LANES), lambda b, h, i: (b, 0))],
        out_specs=pl.BlockSpec((tq, LANES), lambda b, h, i: (b * nq + i, h)),
        out_shape=jax.ShapeDtypeStruct((M, W), BF16),
        scratch_shapes=[pltpu.VMEM((T, 2 * LANES), BF16), pltpu.VMEM((nk, LANES, tk), BF16),
                        pltpu.VMEM((LANES, tq), F32), pltpu.VMEM((2, tq), F32), pltpu.VMEM((2, tq), F32),
                        pltpu.VMEM((2, 2, tk, tq), F32), pltpu.VMEM((2, 2, tq), F32)],
        compiler_params=pltpu.CompilerParams(dimension_semantics=("parallel", "parallel", "arbitrary"),
                                             vmem_limit_bytes=limit),
        name="fox_attn",
    )(qkvg, qkvg, qkvg, qkvg, c2)


def _pick(n, prefs):
    for t in prefs:
        if n % t == 0:
            return t
    raise ValueError(f"no tile in {prefs} divides {n}")


def kernel(x, positions, mix_pre_g, mix_post_g, ffn_pre_g, ffn_post_g, hgrn_w_in, hgrn_w_out, hgrn_norm_g, hgrn_lb_param, swa_w_in, swa_w_out, swa_sinks, sc_w_in, sc_conv_w, sc_w_out, fox_w_in, fox_b_f, fox_w_out, ffn_w_up, ffn_conv_w, ffn_conv_b, ffn_w_down):
    B, T, D = x.shape
    M = B * T
    depth = mix_pre_g.shape[0]
    xf = x.reshape(M, D)
    tm_proj = _pick(M, (1024, 512, 256, 128))
    tm_out = _pick(M, (512, 256, 128))
    tm_ffn = _pick(T, (512, 256, 128))
    bf = lambda w: w.astype(BF16)
    tables = None

    for i in range(depth):
        m, j = i % N_MIXERS, i // N_MIXERS
        if m == 0:
            proj = _norm_matmul(xf, mix_pre_g[i], bf(hgrn_w_in[j]), tm=tm_proj, tn=None)
            a = _hgrn(proj, B, T, hgrn_lb_param, hgrn_norm_g[j], layer=i, tc=_pick(T, (512, 256, 128, 64)),
                      heads_per_step=4)
            w_out = hgrn_w_out[j]
        elif m == 1:
            n_q = swa_sinks.shape[1]
            n_kv = n_q // SWA_GROUP
            if tables is None:
                tables = _rope_tables(positions)
            proj = _norm_matmul(xf, mix_pre_g[i], bf(swa_w_in[j]), tm=tm_proj, tn=None)
            a = _swa(proj, B, T, swa_sinks[j], tables, n_q=n_q, n_kv=n_kv)
            w_out = swa_w_out[j]
        elif m == 2:
            proj = _norm_matmul(xf, mix_pre_g[i], bf(sc_w_in[j]), tm=tm_proj, tn=None)
            a = _sconv(proj, B, T, sc_conv_w[j], tt=_pick(T, (256, 128)))
            w_out = sc_w_out[j]
        else:
            n_heads = fox_b_f.shape[1]
            W = n_heads * FOX_HEAD_DIM
            w_in = fox_w_in[j]
            w_qkvg = bf(jnp.concatenate([w_in[:, :3 * W], w_in[:, 3 * W + n_heads:]], axis=1))
            w_f = bf(jnp.pad(w_in[:, 3 * W:3 * W + n_heads], ((0, 0), (0, LANES - n_heads))))
            b_f = jnp.pad(fox_b_f[j], (0, LANES - n_heads))
            qkvg = _norm_matmul(xf, mix_pre_g[i], w_qkvg, tm=tm_proj, tn=None)
            f_logit = _norm_matmul(xf, mix_pre_g[i], w_f, tm=tm_proj, tn=LANES)
            c2 = _fox_cum(f_logit, B, T, b_f, tb=_pick(T, (256, 128)))
            a = _fox_attn(qkvg, B, T, c2, n_heads=n_heads, tq=1024, tk=512)
            w_out = fox_w_out[j]
        xf = _out_proj(a, bf(w_out), mix_post_g[i], xf, tm=tm_out)
        xf = _ffn(xf, B, T, ffn_pre_g[i], bf(ffn_w_up[i]), ffn_conv_w[i], ffn_conv_b[i], bf(ffn_w_down[i]),
                  ffn_post_g[i], tm=tm_ffn, tf=_pick(ffn_w_down.shape[1], (512, 256, 128)))
    return xf.reshape(B, T, D)
```

```python
import functools
import math

import numpy as np
import jax
import jax.numpy as jnp
from jax import lax
from jax.experimental import pallas as pl
from jax.experimental.pallas import tpu as pltpu

F32 = jnp.float32
BF16 = jnp.bfloat16

RMS_EPS = 1e-6
N_MIXERS = 4
HGRN_HEAD_DIM = 128
HGRN_CHUNK = 128
SWA_HEAD_DIM = 64
SWA_GROUP = 8
SWA_WINDOW = 128
ROPE_THETA = 500000.0
ROT_DIM = 16
FOX_HEAD_DIM = 64
FFN_CONV_WIDTH = 3
FFN_GATE_ROWS = 128
LOG2E = math.log2(math.e)

LANES = 128
SUBLANES = 8
V7X_VMEM_BYTES = 64 * 2**20
VMEM_CAP_BYTES = 56 * 2**20


def _vmem_limit(*nbytes):
    return int(min(VMEM_CAP_BYTES, sum(nbytes)))


def _nbytes(shape, dtype):
    return int(np.prod(shape)) * jnp.dtype(dtype).itemsize


def _rms(x, g):
    ms = jnp.mean(x * x, axis=-1, keepdims=True)
    return x * lax.rsqrt(ms + RMS_EPS) * g


def _dot(a, b):
    return jnp.dot(a, b, preferred_element_type=F32)


def _dot_nt(a, b):
    return lax.dot_general(a, b, (((1,), (1,)), ((), ())), preferred_element_type=F32)


def _dot_tn(a, b):
    return lax.dot_general(a, b, (((0,), (0,)), ((), ())), preferred_element_type=F32)


def _dot_01_f32(m01, x, pieces=3):
    n = x.shape[1]
    terms, rest = [], x
    for _ in range(pieces):
        t = rest.astype(BF16)
        terms.append(t)
        rest = rest - t.astype(F32)
    y = _dot(m01, jnp.concatenate(terms, axis=1))
    out = y[:, :n]
    for i in range(1, pieces):
        out = out + y[:, i * n:(i + 1) * n]
    return out


def _sigmoid(x):
    return 1.0 / (1.0 + jnp.exp(-x))


def _silu(x):
    return x * _sigmoid(x)


def _causal_conv3(u, prev8, w):
    w0, w1, w2 = w[0:1], w[1:2], w[2:3]
    r1 = pltpu.roll(u, 1, 0)
    r2 = pltpu.roll(u, 2, 0)
    y = w2 * u + w1 * r1 + w0 * r2
    rows = lax.broadcasted_iota(jnp.int32, prev8.shape, 0)
    h1 = jnp.where(rows < 1, pltpu.roll(prev8, 1, 0), r1[:SUBLANES])
    h2 = jnp.where(rows < 2, pltpu.roll(prev8, 2, 0), r2[:SUBLANES])
    yh = w2 * u[:SUBLANES] + w1 * h1 + w0 * h2
    return jnp.concatenate([yh, y[SUBLANES:]], axis=0)


def _norm_matmul_kernel(x_ref, g_ref, w_ref, o_ref, hn_ref):
    @pl.when(pl.program_id(1) == 0)
    def _():
        hn_ref[...] = _rms(x_ref[...], g_ref[...]).astype(BF16)

    o_ref[...] = _dot(hn_ref[...], w_ref[...]).astype(o_ref.dtype)


def _norm_matmul(x, g, w, *, tm, tn, out_dtype=F32):
    M, D = x.shape
    N = w.shape[1]
    if tn is None:
        tn = next(t for t in (1024, 1280, 768, 512, 256, LANES) if N % t == 0)
    assert M % tm == 0 and N % tn == 0
    limit = _vmem_limit(2 * _nbytes((tm, D), F32), _nbytes((tm, D), BF16), 2 * _nbytes((D, tn), BF16),
                        3 * _nbytes((tm, tn), F32), _nbytes((tm, D), F32), 4 * 2**20)
    return pl.pallas_call(
        _norm_matmul_kernel,
        grid=(M // tm, N // tn),
        in_specs=[pl.BlockSpec((tm, D), lambda i, j: (i, 0)),
                  pl.BlockSpec((1, D), lambda i, j: (0, 0)),
                  pl.BlockSpec((D, tn), lambda i, j: (0, j))],
        out_specs=pl.BlockSpec((tm, tn), lambda i, j: (i, j)),
        out_shape=jax.ShapeDtypeStruct((M, N), out_dtype),
        scratch_shapes=[pltpu.VMEM((tm, D), BF16)],
        compiler_params=pltpu.CompilerParams(dimension_semantics=("parallel", "arbitrary"),
                                             vmem_limit_bytes=limit),
        name="norm_matmul",
    )(x, g.reshape(1, D), w)


def _out_proj_kernel(a_ref, w_ref, g_ref, x_ref, o_ref):
    y = _dot(a_ref[...], w_ref[...])
    o_ref[...] = x_ref[...] + _rms(y, g_ref[...])


def _out_proj(a, w, g, x, *, tm):
    M, K = a.shape
    D = w.shape[1]
    assert M % tm == 0
    limit = _vmem_limit(2 * _nbytes((tm, K), BF16), 2 * _nbytes((K, D), BF16), 4 * _nbytes((tm, D), F32),
                        2 * _nbytes((tm, D), F32), 4 * 2**20)
    return pl.pallas_call(
        _out_proj_kernel,
        grid=(M // tm,),
        in_specs=[pl.BlockSpec((tm, K), lambda i: (i, 0)),
                  pl.BlockSpec((K, D), lambda i: (0, 0)),
                  pl.BlockSpec((1, D), lambda i: (0, 0)),
                  pl.BlockSpec((tm, D), lambda i: (i, 0))],
        out_specs=pl.BlockSpec((tm, D), lambda i: (i, 0)),
        out_shape=jax.ShapeDtypeStruct((M, D), F32),
        compiler_params=pltpu.CompilerParams(dimension_semantics=("parallel",), vmem_limit_bytes=limit),
        name="out_proj",
    )(a, w, g.reshape(1, D), x)


def _ffn_kernel(x_ref, gpre_ref, w_ref, cwb_ref, wd_ref, gpost_ref, o_ref, hn_ref, acc_ref, carry_ref, act_ref):
    ti = pl.program_id(1)
    c = pl.program_id(2)
    nc = pl.num_programs(2)
    tm = x_ref.shape[0]
    tf = w_ref.shape[3]
    R = FFN_GATE_ROWS
    K = FFN_CONV_WIDTH

    @pl.when(c == 0)
    def _():
        hn_ref[...] = _rms(x_ref[...], gpre_ref[...]).astype(BF16)
        acc_ref[...] = jnp.zeros(acc_ref.shape, F32)

    @pl.when(ti == 0)
    def _():
        carry_ref[c] = jnp.zeros(carry_ref.shape[1:], F32)

    hn = hn_ref[...]
    u = (_dot(hn, w_ref[0, 0]), _dot(hn, w_ref[0, 1]))
    prev = (carry_ref[c, 0], carry_ref[c, 1])
    carry_ref[c, 0] = u[0][tm - SUBLANES:]
    carry_ref[c, 1] = u[1][tm - SUBLANES:]

    def conv(k, r, cols):
        head = prev[k][:, cols] if r == 0 else u[k][r * R - SUBLANES:r * R, cols]
        blk = jnp.concatenate([head, u[k][r * R:(r + 1) * R, cols]], axis=0)
        w = cwb_ref[0, K * k:K * (k + 1), cols]
        y = w[2:3] * blk + w[1:2] * pltpu.roll(blk, 1, 0) + w[0:1] * pltpu.roll(blk, 2, 0)
        return y[SUBLANES:] + cwb_ref[0, 2 * K + k:2 * K + k + 1, cols]

    for j in range(tf // LANES):
        cols = slice(j * LANES, (j + 1) * LANES)
        for r in range(tm // R):
            act_ref[r * R:(r + 1) * R, cols] = (_silu(conv(0, r, cols)) * conv(1, r, cols)).astype(BF16)
    acc_ref[...] += _dot(act_ref[...], wd_ref[...])

    @pl.when(c == nc - 1)
    def _():
        o_ref[...] = x_ref[...] + _rms(acc_ref[...], gpost_ref[...])


def _ffn(x, B, T, g_pre, w_up, conv_w, conv_b, w_down, g_post, *, tm, tf):
    M, D = x.shape
    F = w_down.shape[0]
    assert T % tm == 0 and F % tf == 0 and tm % FFN_GATE_ROWS == 0 and tf % LANES == 0
    nt, nc = T // tm, F // tf
    w_in = w_up.reshape(D, 2, nc, tf).transpose(2, 1, 0, 3)
    cwb = jnp.concatenate([conv_w.reshape(FFN_CONV_WIDTH, 2, nc, tf).transpose(1, 0, 2, 3).reshape(-1, nc, tf),
                           conv_b.reshape(2, nc, tf)], axis=0).transpose(1, 0, 2)
    assert cwb.shape[1] == SUBLANES
    limit = _vmem_limit(4 * _nbytes((tm, D), F32), _nbytes((tm, D), BF16), _nbytes((tm, D), F32),
                        4 * _nbytes((D, tf), BF16), 2 * _nbytes((tf, D), BF16),
                        8 * _nbytes((tm, tf), F32), _nbytes((tm, D), F32), 4 * 2**20)
    return pl.pallas_call(
        _ffn_kernel,
        grid=(B, nt, nc),
        in_specs=[pl.BlockSpec((tm, D), lambda b, t, c: (b * nt + t, 0)),
                  pl.BlockSpec((1, D), lambda b, t, c: (0, 0)),
                  pl.BlockSpec((1, 2, D, tf), lambda b, t, c: (c, 0, 0, 0)),
                  pl.BlockSpec((1, SUBLANES, tf), lambda b, t, c: (c, 0, 0)),
                  pl.BlockSpec((tf, D), lambda b, t, c: (c, 0)),
                  pl.BlockSpec((1, D), lambda b, t, c: (0, 0))],
        out_specs=pl.BlockSpec((tm, D), lambda b, t, c: (b * nt + t, 0)),
        out_shape=jax.ShapeDtypeStruct((M, D), F32),
        scratch_shapes=[pltpu.VMEM((tm, D), BF16),
                        pltpu.VMEM((tm, D), F32),
                        pltpu.VMEM((nc, 2, SUBLANES, tf), F32),
                        pltpu.VMEM((tm, tf), BF16)],
        compiler_params=pltpu.CompilerParams(dimension_semantics=("parallel", "arbitrary", "arbitrary"),
                                             vmem_limit_bytes=limit),
        name="ffn",
    )(x, g_pre.reshape(1, D), w_in, cwb, w_down, g_post.reshape(1, D))


def _hgrn_sum_matrix(C):
    L = int(math.log2(C))
    t = np.arange(C)[:, None]
    j = np.arange(C)[None, :]
    blocks = [j <= t, j > t]
    for l in range(L):
        half, blk = 1 << l, 2 << l
        mid = t - t % blk + half - 1
        is_query = (t % blk) >= half
        blocks.append(np.where(is_query, (j > mid) & (j <= t), (j > t) & (j <= mid)))
    return np.concatenate(blocks, axis=0).astype(np.float32)


def _hgrn_kernel(q_ref, f_ref, i_ref, g_ref, lbp_ref, ng_ref, sm_ref, o_ref, st_ref, *, layer, C, NH):
    tc = q_ref.shape[0]
    L = int(math.log2(C))
    hd = HGRN_HEAD_DIM

    @pl.when(pl.program_id(2) == 0)
    def _():
        st_ref[...] = jnp.zeros(st_ref.shape, F32)

    p = lbp_ref[...]
    e = jnp.exp(p - jnp.max(p, axis=0, keepdims=True))
    lb_all = jnp.sum(e[:layer + 1], axis=0, keepdims=True) / jnp.sum(e, axis=0, keepdims=True)
    ng = ng_ref[...]
    sm = sm_ref[...]

    row = lax.broadcasted_iota(jnp.int32, (C, C), 0)
    col = lax.broadcasted_iota(jnp.int32, (C, C), 1)
    level_mask = [((row >> (l + 1)) == (col >> (l + 1))) & (((row >> l) & 1) == 1) & (((col >> l) & 1) == 0)
                  for l in range(L)]

    def chunk(ci, carry):
        r0 = pl.multiple_of(ci * C, C)
        for hh in range(NH):
            cols = slice(hh * hd, (hh + 1) * hd)
            lb = lb_all[:, cols]
            q = _silu(q_ref[pl.ds(r0, C), cols])
            f = lb + (1.0 - lb) * _sigmoid(f_ref[pl.ds(r0, C), cols])
            k = 1.0 - f
            v = i_ref[pl.ds(r0, C), cols].astype(BF16)
            z = _dot_01_f32(sm, jnp.log(f), pieces=2)
            b = z[:C]
            sfx = z[C:2 * C]
            st = st_ref[hh]
            inter = _dot_nt((q * jnp.exp(b)).astype(BF16), st.astype(BF16))
            a = jnp.where(row == col, jnp.sum(q * k, axis=-1, keepdims=True), 0.0)
            for l in range(L):
                el = jnp.exp(z[(2 + l) * C:(3 + l) * C])
                al = _dot_nt((q * el).astype(BF16), (k * el).astype(BF16))
                a = a + jnp.where(level_mask[l], al, 0.0)
            o = inter + _dot(a.astype(BF16), v)
            st_ref[hh] = jnp.exp(b[C - 1:C]) * st + _dot_tn(v, (k * jnp.exp(sfx)).astype(BF16))
            o = _rms(o, ng) * _silu(g_ref[pl.ds(r0, C), cols])
            o_ref[pl.ds(r0, C), cols] = o.astype(o_ref.dtype)
        return carry

    lax.fori_loop(0, tc // C, chunk, 0, unroll=2)


def _hgrn(proj, B, T, lb_param, norm_g, *, layer, tc, heads_per_step):
    M, W4 = proj.shape
    W = W4 // 4
    H = W // HGRN_HEAD_DIM
    C = HGRN_CHUNK
    NH = heads_per_step
    assert T % tc == 0 and tc % C == 0 and H % NH == 0
    nt = T // tc
    HG = H // NH
    sm = jnp.asarray(_hgrn_sum_matrix(C), BF16)
    hd = HGRN_HEAD_DIM
    bw = NH * hd
    blk = lambda off: pl.BlockSpec((tc, bw), lambda b, h, t, off=off: (b * nt + t, off * HG + h))
    return pl.pallas_call(
        functools.partial(_hgrn_kernel, layer=layer, C=C, NH=NH),
        grid=(B, HG, nt),
        in_specs=[blk(0), blk(1), blk(2), blk(3),
                  pl.BlockSpec((lb_param.shape[0], bw), lambda b, h, t: (0, h)),
                  pl.BlockSpec((1, hd), lambda b, h, t: (0, 0)),
                  pl.BlockSpec(sm.shape, lambda b, h, t: (0, 0))],
        out_specs=pl.BlockSpec((tc, bw), lambda b, h, t: (b * nt + t, h)),
        out_shape=jax.ShapeDtypeStruct((M, W), BF16),
        scratch_shapes=[pltpu.VMEM((NH, hd, hd), F32)],
        compiler_params=pltpu.CompilerParams(dimension_semantics=("parallel", "parallel", "arbitrary")),
        name="hgrn2",
    )(proj, proj, proj, proj, lb_param, norm_g.reshape(1, hd), sm)


def _rope_table_kernel(pos_ref, cs_ref, sup_ref, sdn_ref):
    half = ROT_DIM // 2
    lane = lax.broadcasted_iota(jnp.int32, (1, LANES), 1)
    d = lane % SWA_HEAD_DIM
    inv_freq = jnp.exp((d % half).astype(F32) * (-math.log(ROPE_THETA) / half))
    ang = pos_ref[...] * inv_freq
    cos, sin = jnp.cos(ang), jnp.sin(ang)
    cs_ref[...] = jnp.where(d < ROT_DIM, cos, 1.0)
    sup_ref[...] = jnp.where(d < half, -sin, 0.0)
    sdn_ref[...] = jnp.where((d >= half) & (d < ROT_DIM), sin, 0.0)


def _rope_tables(positions):
    T = positions.shape[0]
    out = jax.ShapeDtypeStruct((T, LANES), F32)
    return pl.pallas_call(
        _rope_table_kernel,
        grid=(1,),
        in_specs=[pl.BlockSpec((T, 1), lambda i: (0, 0))],
        out_specs=[pl.BlockSpec((T, LANES), lambda i: (0, 0))] * 3,
        out_shape=[out, out, out],
        name="rope_tables",
    )(positions.astype(F32).reshape(T, 1))


def _rope(x, cs, sup, sdn):
    w = x.shape[1]
    rep = w // LANES
    tile = lambda a: a if rep == 1 else jnp.concatenate([a] * rep, axis=1)
    half = ROT_DIM // 2
    return x * tile(cs) + pltpu.roll(x, w - half, 1) * tile(sup) + pltpu.roll(x, half, 1) * tile(sdn)


def _swa_kernel(sink_ref, q_ref, kc_ref, kp_ref, vc_ref, vp_ref,
                csc_ref, supc_ref, sdnc_ref, csp_ref, supp_ref, sdnp_ref, o_ref, *, n_kv):
    n = pl.program_id(1)
    W = SWA_WINDOW
    G = SWA_GROUP
    lane = lax.broadcasted_iota(jnp.int32, (1, LANES), 1)
    lo_half = lane < SWA_HEAD_DIM

    q = _rope(q_ref[...], csc_ref[...], supc_ref[...], sdnc_ref[...]) * (SWA_HEAD_DIM ** -0.5 * LOG2E)
    k = jnp.concatenate([_rope(kp_ref[...], csp_ref[...], supp_ref[...], sdnp_ref[...]),
                         _rope(kc_ref[...], csc_ref[...], supc_ref[...], sdnc_ref[...])], axis=0)
    v = jnp.concatenate([vp_ref[...], vc_ref[...]], axis=0)

    S = (G // 2) * W
    kj = lax.broadcasted_iota(jnp.int32, (2 * W, S), 0)
    qi = lax.broadcasted_iota(jnp.int32, (2 * W, S), 1) & (W - 1)
    diff = qi + W - kj
    allowed = (diff >= 0) & (diff < W) & ((kj >= W) | (n > 0))
    mask_bias = jnp.where(allowed, 0.0, -jnp.inf)
    half_mask = [lo_half, jnp.logical_not(lo_half)]
    hd = SWA_HEAD_DIM

    for jp in range(n_kv // 2):
        pair = slice(jp * LANES, (jp + 1) * LANES)
        kpair = k[:, pair]
        k_same = kpair.astype(BF16)
        k_swap = pltpu.roll(kpair, hd, 1).astype(BF16)
        vt = v[:, pair].T.astype(BF16)
        for jj in range(2):
            j = 2 * jp + jj
            outs_t = []
            for hh in range(2):
                heads = [j * G + g for g in range(hh, G, 2)]
                qs = jnp.concatenate(
                    [jnp.where(half_mask[hh], q[:, (h // 2) * LANES:(h // 2 + 1) * LANES], 0.0) for h in heads],
                    axis=0).astype(BF16)
                st = _dot_nt(k_same if hh == jj else k_swap, qs) + mask_bias
                sink = jnp.concatenate([jnp.full((1, W), sink_ref[h] * LOG2E, F32) for h in heads], axis=1)
                m = jnp.maximum(jnp.max(st, axis=0, keepdims=True), sink)
                e = jnp.exp2(st - m)
                denom = jnp.sum(e, axis=0, keepdims=True) + jnp.exp2(sink - m)
                outs_t.append(_dot(vt[jj * hd:(jj + 1) * hd, :], e.astype(BF16)) / denom)
            for gp in range(G // 2):
                pr = j * (G // 2) + gp
                cols = slice(gp * W, (gp + 1) * W)
                ot = jnp.concatenate([outs_t[0][:, cols], outs_t[1][:, cols]], axis=0)
                o_ref[:, pr * LANES:(pr + 1) * LANES] = ot.T.astype(o_ref.dtype)


def _swa(proj, B, T, sinks, tables, *, n_q, n_kv):
    M = proj.shape[0]
    W = SWA_WINDOW
    assert T % W == 0 and n_q == n_kv * SWA_GROUP and n_kv % 2 == 0
    nblk = T // W
    qw, kw = n_q * SWA_HEAD_DIM, n_kv * SWA_HEAD_DIM
    assert qw % kw == 0
    kcol = qw // kw
    cs, sup, sdn = tables
    cur = lambda b, n: (b * nblk + n, 0)
    tcur = lambda b, n: (n, 0)
    tprev = lambda b, n: (jnp.maximum(n - 1, 0), 0)
    tspec = lambda im: pl.BlockSpec((W, LANES), im)
    return pl.pallas_call(
        functools.partial(_swa_kernel, n_kv=n_kv),
        grid=(B, nblk),
        in_specs=[pl.BlockSpec(memory_space=pltpu.SMEM),
                  pl.BlockSpec((W, qw), cur),
                  pl.BlockSpec((W, kw), lambda b, n: (b * nblk + n, kcol)),
                  pl.BlockSpec((W, kw), lambda b, n: (b * nblk + jnp.maximum(n - 1, 0), kcol)),
                  pl.BlockSpec((W, kw), lambda b, n: (b * nblk + n, kcol + 1)),
                  pl.BlockSpec((W, kw), lambda b, n: (b * nblk + jnp.maximum(n - 1, 0), kcol + 1)),
                  tspec(tcur), tspec(tcur), tspec(tcur), tspec(tprev), tspec(tprev), tspec(tprev)],
        out_specs=pl.BlockSpec((W, qw), cur),
        out_shape=jax.ShapeDtypeStruct((M, qw), BF16),
        compiler_params=pltpu.CompilerParams(dimension_semantics=("parallel", "parallel")),
        name="swa",
    )(sinks, proj, proj, proj, proj, proj, cs, sup, sdn, cs, sup, sdn)


def _sconv_kernel(b_ref, c_ref, x_ref, w_ref, o_ref, carry_ref):
    tt = b_ref.shape[0]

    @pl.when(pl.program_id(1) == 0)
    def _():
        carry_ref[...] = jnp.zeros(carry_ref.shape, F32)

    u = c_ref[...] * x_ref[...]
    y = _causal_conv3(u, carry_ref[...], w_ref[...])
    carry_ref[...] = u[tt - SUBLANES:]
    o_ref[...] = (b_ref[...] * y).astype(o_ref.dtype)


def _sconv(proj, B, T, conv_w, *, tt):
    M, D3 = proj.shape
    D = D3 // 3
    assert T % tt == 0
    nt = T // tt
    blk = lambda off: pl.BlockSpec((tt, D), lambda b, t, off=off: (b * nt + t, off))
    return pl.pallas_call(
        _sconv_kernel,
        grid=(B, nt),
        in_specs=[blk(0), blk(1), blk(2), pl.BlockSpec(conv_w.shape, lambda b, t: (0, 0))],
        out_specs=pl.BlockSpec((tt, D), lambda b, t: (b * nt + t, 0)),
        out_shape=jax.ShapeDtypeStruct((M, D), BF16),
        scratch_shapes=[pltpu.VMEM((SUBLANES, D), F32)],
        compiler_params=pltpu.CompilerParams(dimension_semantics=("parallel", "arbitrary"),
                                             vmem_limit_bytes=_vmem_limit(12 * _nbytes((tt, D), F32), 4 * 2**20)),
        name="sconv",
    )(proj, proj, proj, conv_w)


def _fox_cum_kernel(f_ref, bf_ref, tri_ref, c_ref, carry_ref):
    tb = f_ref.shape[0]

    @pl.when(pl.program_id(1) == 0)
    def _():
        carry_ref[...] = jnp.zeros(carry_ref.shape, F32)

    x = f_ref[...] + bf_ref[...]
    log_f = jnp.minimum(x, 0.0) - jnp.log(1.0 + jnp.exp(-jnp.abs(x)))
    c = _dot_01_f32(tri_ref[...], log_f) + carry_ref[...]
    carry_ref[...] = c[tb - 1:tb]
    c_ref[...] = c * LOG2E


def _fox_cum(f_logit, B, T, b_f, *, tb):
    assert T % tb == 0
    nt = T // tb
    tri = jnp.asarray(np.tril(np.ones((tb, tb), np.float32)), BF16)
    return pl.pallas_call(
        _fox_cum_kernel,
        grid=(B, nt),
        in_specs=[pl.BlockSpec((tb, LANES), lambda b, t: (b * nt + t, 0)),
                  pl.BlockSpec((1, LANES), lambda b, t: (0, 0)),
                  pl.BlockSpec((tb, tb), lambda b, t: (0, 0))],
        out_specs=pl.BlockSpec((tb, LANES), lambda b, t: (b * nt + t, 0)),
        out_shape=jax.ShapeDtypeStruct((B * T, LANES), F32),
        scratch_shapes=[pltpu.VMEM((1, LANES), F32)],
        compiler_params=pltpu.CompilerParams(dimension_semantics=("parallel", "arbitrary")),
        name="fox_cum",
    )(f_logit, b_f.reshape(1, LANES), tri)


def _fox_kernel(q_ref, k_ref, v_ref, g_ref, c_ref, o_ref, kx_ref, vt_ref, acc_ref, m_ref, l_ref, s_ref, mb_ref,
                *, tq, tk):
    h = pl.program_id(1)
    qi = pl.program_id(2)
    nk = vt_ref.shape[0]
    hd = FOX_HEAD_DIM

    @pl.when(qi == 0)
    def _():
        r = lax.broadcasted_iota(jnp.int32, (LANES, LANES), 0)
        cc = lax.broadcasted_iota(jnp.int32, (LANES, LANES), 1)

        def sel(piece):
            hit = ((cc == piece) & (r == 2 * h)) | ((cc == piece + 3) & (r == 2 * h + 1))
            return jnp.where(hit, 1.0, 0.0).astype(BF16)

        s_hi, s_mid, s_lo = sel(0), sel(1), sel(2)
        for j in range(nk):
            rows = slice(j * tk, (j + 1) * tk)
            kx_ref[rows, :LANES] = k_ref[rows, :].astype(BF16)
            c = c_ref[rows, :]
            hi = c.astype(BF16)
            r1 = c - hi.astype(F32)
            mid = r1.astype(BF16)
            lo = (r1 - mid.astype(F32)).astype(BF16)
            kx_ref[rows, LANES:] = (_dot(hi, s_hi) + _dot(mid, s_mid) + _dot(lo, s_lo)).astype(BF16)
            vt_ref[j] = v_ref[rows, :].T.astype(BF16)

    lane = lax.broadcasted_iota(jnp.int32, (1, LANES), 1)
    lo_half = lane < hd
    q2 = q_ref[...] * (hd ** -0.5 * LOG2E)
    qx = []
    for hh in range(2):
        qm = jnp.where(lo_half if hh == 0 else jnp.logical_not(lo_half), q2, 0.0)
        minus_one = jnp.where((lane >= 3 * hh) & (lane < 3 * hh + 3), -1.0, 0.0)
        qx.append(jnp.concatenate([qm, jnp.broadcast_to(minus_one, (tq, LANES))], axis=1).astype(BF16))

    m_ref[...] = jnp.full(m_ref.shape, -jnp.inf, F32)
    l_ref[...] = jnp.zeros(l_ref.shape, F32)
    acc_ref[...] = jnp.zeros(acc_ref.shape, F32)
    key_i = lax.broadcasted_iota(jnp.int32, (tk, tq), 0)
    qry_i = lax.broadcasted_iota(jnp.int32, (tk, tq), 1)

    def produce(j, buf, diag):
        kxb = kx_ref[pl.ds(pl.multiple_of(j * tk, tk), tk), :]
        for hh in range(2):
            st = _dot_nt(kxb, qx[hh])
            if diag is not None:
                st = jnp.where(key_i + diag * tk <= qry_i, st, -jnp.inf)
            s_ref[buf, hh] = st
            mb_ref[buf, hh:hh + 1, :] = jnp.max(st, axis=0, keepdims=True)

    def consume(j, buf):
        vtb = vt_ref[j]
        for hh in range(2):
            m_old = m_ref[hh:hh + 1, :]
            m_new = jnp.maximum(m_old, mb_ref[buf, hh:hh + 1, :])
            alpha = jnp.exp2(m_old - m_new)
            p = jnp.exp2(s_ref[buf, hh] - m_new)
            l_ref[hh:hh + 1, :] = alpha * l_ref[hh:hh + 1, :] + jnp.sum(p, axis=0, keepdims=True)
            rows = slice(hh * hd, (hh + 1) * hd)
            acc_ref[rows, :] = alpha * acc_ref[rows, :] + _dot(vtb[rows, :], p.astype(BF16))
            m_ref[hh:hh + 1, :] = m_new

    assert tq == 2 * tk

    def pair(i, next_is_diagonal):
        produce(2 * i + 1, 1, None)
        consume(2 * i, 0)
        produce(2 * i + 2, 0, 0 if next_is_diagonal else None)
        consume(2 * i + 1, 1)

    def full_pair(i, carry):
        pair(i, False)
        return carry

    def finish():
        produce(2 * qi + 1, 1, 1)
        consume(2 * qi, 0)
        consume(2 * qi + 1, 1)
        ot = jnp.concatenate([acc_ref[:hd, :] / l_ref[0:1, :], acc_ref[hd:, :] / l_ref[1:2, :]], axis=0)
        o_ref[...] = (ot.T * _sigmoid(g_ref[...])).astype(o_ref.dtype)

    @pl.when(qi == 0)
    def _():
        produce(0, 0, 0)
        finish()

    @pl.when(qi > 0)
    def _():
        produce(0, 0, None)
        lax.fori_loop(0, qi - 1, full_pair, 0)
        pair(qi - 1, True)
        finish()


def _fox_attn(qkvg, B, T, c2, *, n_heads, tq, tk):
    M, W4 = qkvg.shape
    W = W4 // 4
    assert W == n_heads * FOX_HEAD_DIM and T % tq == 0 and tq % tk == 0 and n_heads % 2 == 0
    hp = n_heads // 2
    nq, nk = T // tq, T // tk
    limit = _vmem_limit(6 * _nbytes((T, LANES), F32), _nbytes((T, 2 * LANES), BF16), _nbytes((T, LANES), BF16),
                        16 * _nbytes((tk, tq), F32), 8 * 2**20)
    return pl.pallas_call(
        functools.partial(_fox_kernel, tq=tq, tk=tk),
        grid=(B, hp, nq),
        in_specs=[pl.BlockSpec((tq, LANES), lambda b, h, i: (b * nq + i, h)),
                  pl.BlockSpec((T, LANES), lambda b, h, i: (b, hp + h)),
                  pl.BlockSpec((T, LANES), lambda b, h, i: (b, 2 * hp + h)),
                  pl.BlockSpec((tq, LANES), lambda b, h, i: (b * nq + i, 3 * hp + h)),
                  pl.BlockSpec((T, LANES), lambda b, h, i: (b, 0))],
        out_specs=pl.BlockSpec((tq, LANES), lambda b, h, i: (b * nq + i, h)),
        out_shape=jax.ShapeDtypeStruct((M, W), BF16),
        scratch_shapes=[pltpu.VMEM((T, 2 * LANES), BF16), pltpu.VMEM((nk, LANES, tk), BF16),
                        pltpu.VMEM((LANES, tq), F32), pltpu.VMEM((2, tq), F32), pltpu.VMEM((2, tq), F32),
                        pltpu.VMEM((2, 2, tk, tq), F32), pltpu.VMEM((2, 2, tq), F32)],
        compiler_params=pltpu.CompilerParams(dimension_semantics=("parallel", "parallel", "arbitrary"),
                                             vmem_limit_bytes=limit),
        name="fox_attn",
    )(qkvg, qkvg, qkvg, qkvg, c2)


def _pick(n, prefs):
    for t in prefs:
        if n % t == 0:
            return t
    raise ValueError(f"no tile in {prefs} divides {n}")


def kernel(x, positions, mix_pre_g, mix_post_g, ffn_pre_g, ffn_post_g, hgrn_w_in, hgrn_w_out, hgrn_norm_g, hgrn_lb_param, swa_w_in, swa_w_out, swa_sinks, sc_w_in, sc_conv_w, sc_w_out, fox_w_in, fox_b_f, fox_w_out, ffn_w_up, ffn_conv_w, ffn_conv_b, ffn_w_down):
    B, T, D = x.shape
    M = B * T
    depth = mix_pre_g.shape[0]
    xf = x.reshape(M, D)
    tm_proj = _pick(M, (1024, 512, 256, 128))
    tm_out = _pick(M, (512, 256, 128))
    tm_ffn = _pick(T, (512, 256, 128))
    bf = lambda w: w.astype(BF16)
    tables = None

    for i in range(depth):
        m, j = i % N_MIXERS, i // N_MIXERS
        if m == 0:
            proj = _norm_matmul(xf, mix_pre_g[i], bf(hgrn_w_in[j]), tm=tm_proj, tn=None)
            a = _hgrn(proj, B, T, hgrn_lb_param, hgrn_norm_g[j], layer=i, tc=_pick(T, (512, 256, 128, 64)),
                      heads_per_step=8)
            w_out = hgrn_w_out[j]
        elif m == 1:
            n_q = swa_sinks.shape[1]
            n_kv = n_q // SWA_GROUP
            if tables is None:
                tables = _rope_tables(positions)
            proj = _norm_matmul(xf, mix_pre_g[i], bf(swa_w_in[j]), tm=tm_proj, tn=None)
            a = _swa(proj, B, T, swa_sinks[j], tables, n_q=n_q, n_kv=n_kv)
            w_out = swa_w_out[j]
        elif m == 2:
            proj = _norm_matmul(xf, mix_pre_g[i], bf(sc_w_in[j]), tm=tm_proj, tn=None)
            a = _sconv(proj, B, T, sc_conv_w[j], tt=_pick(T, (256, 128)))
            w_out = sc_w_out[j]
        else:
            n_heads = fox_b_f.shape[1]
            W = n_heads * FOX_HEAD_DIM
            w_in = fox_w_in[j]
            w_qkvg = bf(jnp.concatenate([w_in[:, :3 * W], w_in[:, 3 * W + n_heads:]], axis=1))
            w_f = bf(jnp.pad(w_in[:, 3 * W:3 * W + n_heads], ((0, 0), (0, LANES - n_heads))))
            b_f = jnp.pad(fox_b_f[j], (0, LANES - n_heads))
            qkvg = _norm_matmul(xf, mix_pre_g[i], w_qkvg, tm=tm_proj, tn=None)
            f_logit = _norm_matmul(xf, mix_pre_g[i], w_f, tm=tm_proj, tn=LANES)
            c2 = _fox_cum(f_logit, B, T, b_f, tb=_pick(T, (256, 128)))
            a = _fox_attn(qkvg, B, T, c2, n_heads=n_heads, tq=1024, tk=512)
            w_out = fox_w_out[j]
        xf = _out_proj(a, bf(w_out), mix_post_g[i], xf, tm=tm_out)
        xf = _ffn(xf, B, T, ffn_pre_g[i], bf(ffn_w_up[i]), ffn_conv_w[i], ffn_conv_b[i], bf(ffn_w_down[i]),
                  ffn_post_g[i], tm=tm_ffn, tf=_pick(ffn_w_down.shape[1], (512, 256, 128)))
    return xf.reshape(B, T, D)
```

```python
import functools
import math

import numpy as np
import jax
import jax.numpy as jnp
from jax import lax
from jax.experimental import pallas as pl
from jax.experimental.pallas import tpu as pltpu

F32 = jnp.float32
BF16 = jnp.bfloat16

RMS_EPS = 1e-6
N_MIXERS = 4
HGRN_HEAD_DIM = 128
HGRN_CHUNK = 128
SWA_HEAD_DIM = 64
SWA_GROUP = 8
SWA_WINDOW = 128
ROPE_THETA = 500000.0
ROT_DIM = 16
FOX_HEAD_DIM = 64
FFN_CONV_WIDTH = 3
FFN_GATE_ROWS = 128
LOG2E = math.log2(math.e)

LANES = 128
SUBLANES = 8
V7X_VMEM_BYTES = 64 * 2**20
VMEM_CAP_BYTES = V7X_VMEM_BYTES - 8 * 2**20


def _vmem_limit(*nbytes):
    return int(min(VMEM_CAP_BYTES, sum(nbytes)))


def _nbytes(shape, dtype):
    return int(np.prod(shape)) * jnp.dtype(dtype).itemsize


def _rms(x, g):
    ms = jnp.mean(x * x, axis=-1, keepdims=True)
    return x * lax.rsqrt(ms + RMS_EPS) * g


def _dot(a, b):
    return jnp.dot(a, b, preferred_element_type=F32)


def _dot_nt(a, b):
    return lax.dot_general(a, b, (((1,), (1,)), ((), ())), preferred_element_type=F32)


def _dot_tn(a, b):
    return lax.dot_general(a, b, (((0,), (0,)), ((), ())), preferred_element_type=F32)


def _dot_01_f32(m01, x, pieces=3):
    n = x.shape[1]
    terms, rest = [], x
    for _ in range(pieces):
        t = rest.astype(BF16)
        terms.append(t)
        rest = rest - t.astype(F32)
    y = _dot(m01, jnp.concatenate(terms, axis=1))
    out = y[:, :n]
    for i in range(1, pieces):
        out = out + y[:, i * n:(i + 1) * n]
    return out


def _sigmoid(x):
    return 1.0 / (1.0 + jnp.exp(-x))


def _silu(x):
    return x * _sigmoid(x)


def _causal_conv3(u, prev8, w):
    w0, w1, w2 = w[0:1], w[1:2], w[2:3]
    r1 = pltpu.roll(u, 1, 0)
    r2 = pltpu.roll(u, 2, 0)
    y = w2 * u + w1 * r1 + w0 * r2
    rows = lax.broadcasted_iota(jnp.int32, prev8.shape, 0)
    h1 = jnp.where(rows < 1, pltpu.roll(prev8, 1, 0), r1[:SUBLANES])
    h2 = jnp.where(rows < 2, pltpu.roll(prev8, 2, 0), r2[:SUBLANES])
    yh = w2 * u[:SUBLANES] + w1 * h1 + w0 * h2
    return jnp.concatenate([yh, y[SUBLANES:]], axis=0)


def _norm_matmul_kernel(x_ref, g_ref, w_ref, o_ref, hn_ref):
    @pl.when(pl.program_id(1) == 0)
    def _():
        hn_ref[...] = _rms(x_ref[...], g_ref[...]).astype(BF16)

    o_ref[...] = _dot(hn_ref[...], w_ref[...]).astype(o_ref.dtype)


def _norm_matmul(x, g, w, *, tm, tn, out_dtype=F32):
    M, D = x.shape
    N = w.shape[1]
    if tn is None:
        tn = next(t for t in (1024, 1280, 768, 512, 256, LANES) if N % t == 0)
    assert M % tm == 0 and N % tn == 0
    limit = _vmem_limit(2 * _nbytes((tm, D), F32), _nbytes((tm, D), BF16), 2 * _nbytes((D, tn), BF16),
                        3 * _nbytes((tm, tn), F32), _nbytes((tm, D), F32), 4 * 2**20)
    return pl.pallas_call(
        _norm_matmul_kernel,
        grid=(M // tm, N // tn),
        in_specs=[pl.BlockSpec((tm, D), lambda i, j: (i, 0)),
                  pl.BlockSpec((1, D), lambda i, j: (0, 0)),
                  pl.BlockSpec((D, tn), lambda i, j: (0, j))],
        out_specs=pl.BlockSpec((tm, tn), lambda i, j: (i, j)),
        out_shape=jax.ShapeDtypeStruct((M, N), out_dtype),
        scratch_shapes=[pltpu.VMEM((tm, D), BF16)],
        compiler_params=pltpu.CompilerParams(dimension_semantics=("parallel", "arbitrary"),
                                             vmem_limit_bytes=limit),
        name="norm_matmul",
    )(x, g.reshape(1, D), w)


def _out_proj_kernel(a_ref, w_ref, g_ref, x_ref, o_ref):
    y = _dot(a_ref[...], w_ref[...])
    o_ref[...] = x_ref[...] + _rms(y, g_ref[...])


def _out_proj(a, w, g, x, *, tm):
    M, K = a.shape
    D = w.shape[1]
    assert M % tm == 0
    limit = _vmem_limit(2 * _nbytes((tm, K), BF16), 2 * _nbytes((K, D), BF16), 4 * _nbytes((tm, D), F32),
                        2 * _nbytes((tm, D), F32), 4 * 2**20)
    return pl.pallas_call(
        _out_proj_kernel,
        grid=(M // tm,),
        in_specs=[pl.BlockSpec((tm, K), lambda i: (i, 0)),
                  pl.BlockSpec((K, D), lambda i: (0, 0)),
                  pl.BlockSpec((1, D), lambda i: (0, 0)),
                  pl.BlockSpec((tm, D), lambda i: (i, 0))],
        out_specs=pl.BlockSpec((tm, D), lambda i: (i, 0)),
        out_shape=jax.ShapeDtypeStruct((M, D), F32),
        compiler_params=pltpu.CompilerParams(dimension_semantics=("parallel",), vmem_limit_bytes=limit),
        name="out_proj",
    )(a, w, g.reshape(1, D), x)


def _ffn_kernel(x_ref, gpre_ref, w_ref, cwb_ref, wd_ref, gpost_ref, o_ref, hn_ref, acc_ref, carry_ref, act_ref):
    ti = pl.program_id(1)
    c = pl.program_id(2)
    nc = pl.num_programs(2)
    tm = x_ref.shape[0]
    tf = w_ref.shape[3]
    R = FFN_GATE_ROWS
    K = FFN_CONV_WIDTH

    @pl.when(c == 0)
    def _():
        hn_ref[...] = _rms(x_ref[...], gpre_ref[...]).astype(BF16)
        acc_ref[...] = jnp.zeros(acc_ref.shape, F32)

    @pl.when(ti == 0)
    def _():
        carry_ref[c] = jnp.zeros(carry_ref.shape[1:], F32)

    hn = hn_ref[...]
    u = (_dot(hn, w_ref[0, 0]), _dot(hn, w_ref[0, 1]))
    prev = (carry_ref[c, 0], carry_ref[c, 1])
    carry_ref[c, 0] = u[0][tm - SUBLANES:]
    carry_ref[c, 1] = u[1][tm - SUBLANES:]

    def conv(k, r, cols):
        head = prev[k][:, cols] if r == 0 else u[k][r * R - SUBLANES:r * R, cols]
        blk = jnp.concatenate([head, u[k][r * R:(r + 1) * R, cols]], axis=0)
        w = cwb_ref[0, K * k:K * (k + 1), cols]
        y = w[2:3] * blk + w[1:2] * pltpu.roll(blk, 1, 0) + w[0:1] * pltpu.roll(blk, 2, 0)
        return y[SUBLANES:] + cwb_ref[0, 2 * K + k:2 * K + k + 1, cols]

    for j in range(tf // LANES):
        cols = slice(j * LANES, (j + 1) * LANES)
        for r in range(tm // R):
            act_ref[r * R:(r + 1) * R, cols] = (_silu(conv(0, r, cols)) * conv(1, r, cols)).astype(BF16)
    acc_ref[...] += _dot(act_ref[...], wd_ref[...])

    @pl.when(c == nc - 1)
    def _():
        o_ref[...] = x_ref[...] + _rms(acc_ref[...], gpost_ref[...])


def _ffn(x, B, T, g_pre, w_up, conv_w, conv_b, w_down, g_post, *, tm, tf):
    M, D = x.shape
    F = w_down.shape[0]
    assert T % tm == 0 and F % tf == 0 and tm % FFN_GATE_ROWS == 0 and tf % LANES == 0
    nt, nc = T // tm, F // tf
    w_in = w_up.reshape(D, 2, nc, tf).transpose(2, 1, 0, 3)
    cwb = jnp.concatenate([conv_w.reshape(FFN_CONV_WIDTH, 2, nc, tf).transpose(1, 0, 2, 3).reshape(-1, nc, tf),
                           conv_b.reshape(2, nc, tf)], axis=0).transpose(1, 0, 2)
    assert cwb.shape[1] == SUBLANES
    limit = _vmem_limit(4 * _nbytes((tm, D), F32), _nbytes((tm, D), BF16), _nbytes((tm, D), F32),
                        4 * _nbytes((D, tf), BF16), 2 * _nbytes((tf, D), BF16),
                        8 * _nbytes((tm, tf), F32), _nbytes((tm, D), F32), 4 * 2**20)
    return pl.pallas_call(
        _ffn_kernel,
        grid=(B, nt, nc),
        in_specs=[pl.BlockSpec((tm, D), lambda b, t, c: (b * nt + t, 0)),
                  pl.BlockSpec((1, D), lambda b, t, c: (0, 0)),
                  pl.BlockSpec((1, 2, D, tf), lambda b, t, c: (c, 0, 0, 0)),
                  pl.BlockSpec((1, SUBLANES, tf), lambda b, t, c: (c, 0, 0)),
                  pl.BlockSpec((tf, D), lambda b, t, c: (c, 0)),
                  pl.BlockSpec((1, D), lambda b, t, c: (0, 0))],
        out_specs=pl.BlockSpec((tm, D), lambda b, t, c: (b * nt + t, 0)),
        out_shape=jax.ShapeDtypeStruct((M, D), F32),
        scratch_shapes=[pltpu.VMEM((tm, D), BF16),
                        pltpu.VMEM((tm, D), F32),
                        pltpu.VMEM((nc, 2, SUBLANES, tf), F32),
                        pltpu.VMEM((tm, tf), BF16)],
        compiler_params=pltpu.CompilerParams(dimension_semantics=("parallel", "arbitrary", "arbitrary"),
                                             vmem_limit_bytes=limit),
        name="ffn",
    )(x, g_pre.reshape(1, D), w_in, cwb, w_down, g_post.reshape(1, D))


def _hgrn_sum_matrix(C):
    L = int(math.log2(C))
    t = np.arange(C)[:, None]
    j = np.arange(C)[None, :]
    blocks = [j <= t, j > t]
    for l in range(L):
        half, blk = 1 << l, 2 << l
        mid = t - t % blk + half - 1
        is_query = (t % blk) >= half
        blocks.append(np.where(is_query, (j > mid) & (j <= t), (j > t) & (j <= mid)))
    return np.concatenate(blocks, axis=0).astype(np.float32)


def _hgrn_kernel(q_ref, f_ref, i_ref, g_ref, lbp_ref, ng_ref, sm_ref, o_ref, st_ref, *, layer, C, NH):
    tc = q_ref.shape[0]
    L = int(math.log2(C))
    hd = HGRN_HEAD_DIM

    @pl.when(pl.program_id(2) == 0)
    def _():
        st_ref[...] = jnp.zeros(st_ref.shape, F32)

    p = lbp_ref[...]
    e = jnp.exp(p - jnp.max(p, axis=0, keepdims=True))
    lb_all = jnp.sum(e[:layer + 1], axis=0, keepdims=True) / jnp.sum(e, axis=0, keepdims=True)
    ng = ng_ref[...]
    sm = sm_ref[...]

    row = lax.broadcasted_iota(jnp.int32, (C, C), 0)
    col = lax.broadcasted_iota(jnp.int32, (C, C), 1)
    level_mask = [((row >> (l + 1)) == (col >> (l + 1))) & (((row >> l) & 1) == 1) & (((col >> l) & 1) == 0)
                  for l in range(L)]

    def chunk(ci, carry):
        r0 = pl.multiple_of(ci * C, C)
        for hh in range(NH):
            cols = slice(hh * hd, (hh + 1) * hd)
            lb = lb_all[:, cols]
            q = _silu(q_ref[pl.ds(r0, C), cols])
            f = lb + (1.0 - lb) * _sigmoid(f_ref[pl.ds(r0, C), cols])
            k = 1.0 - f
            v = i_ref[pl.ds(r0, C), cols].astype(BF16)
            z = _dot_01_f32(sm, jnp.log(f), pieces=2)
            b = z[:C]
            sfx = z[C:2 * C]
            st = st_ref[hh]
            inter = _dot_nt((q * jnp.exp(b)).astype(BF16), st.astype(BF16))
            a = jnp.where(row == col, jnp.sum(q * k, axis=-1, keepdims=True), 0.0)
            for l in range(L):
                el = jnp.exp(z[(2 + l) * C:(3 + l) * C])
                al = _dot_nt((q * el).astype(BF16), (k * el).astype(BF16))
                a = a + jnp.where(level_mask[l], al, 0.0)
            o = inter + _dot(a.astype(BF16), v)
            st_ref[hh] = jnp.exp(b[C - 1:C]) * st + _dot_tn(v, (k * jnp.exp(sfx)).astype(BF16))
            o = _rms(o, ng) * _silu(g_ref[pl.ds(r0, C), cols])
            o_ref[pl.ds(r0, C), cols] = o.astype(o_ref.dtype)
        return carry

    lax.fori_loop(0, tc // C, chunk, 0, unroll=2)


def _hgrn(proj, B, T, lb_param, norm_g, *, layer, tc, heads_per_step):
    M, W4 = proj.shape
    W = W4 // 4
    H = W // HGRN_HEAD_DIM
    C = HGRN_CHUNK
    NH = heads_per_step
    assert T % tc == 0 and tc % C == 0 and H % NH == 0
    nt = T // tc
    HG = H // NH
    sm = jnp.asarray(_hgrn_sum_matrix(C), BF16)
    hd = HGRN_HEAD_DIM
    bw = NH * hd
    blk = lambda off: pl.BlockSpec((tc, bw), lambda b, h, t, off=off: (b * nt + t, off * HG + h))
    return pl.pallas_call(
        functools.partial(_hgrn_kernel, layer=layer, C=C, NH=NH),
        grid=(B, HG, nt),
        in_specs=[blk(0), blk(1), blk(2), blk(3),
                  pl.BlockSpec((lb_param.shape[0], bw), lambda b, h, t: (0, h)),
                  pl.BlockSpec((1, hd), lambda b, h, t: (0, 0)),
                  pl.BlockSpec(sm.shape, lambda b, h, t: (0, 0))],
        out_specs=pl.BlockSpec((tc, bw), lambda b, h, t: (b * nt + t, h)),
        out_shape=jax.ShapeDtypeStruct((M, W), BF16),
        scratch_shapes=[pltpu.VMEM((NH, hd, hd), F32)],
        compiler_params=pltpu.CompilerParams(dimension_semantics=("parallel", "parallel", "arbitrary")),
        name="hgrn2",
    )(proj, proj, proj, proj, lb_param, norm_g.reshape(1, hd), sm)


def _rope_table_kernel(pos_ref, cs_ref, sup_ref, sdn_ref):
    half = ROT_DIM // 2
    lane = lax.broadcasted_iota(jnp.int32, (1, LANES), 1)
    d = lane % SWA_HEAD_DIM
    inv_freq = jnp.exp((d % half).astype(F32) * (-math.log(ROPE_THETA) / half))
    ang = pos_ref[...] * inv_freq
    cos, sin = jnp.cos(ang), jnp.sin(ang)
    cs_ref[...] = jnp.where(d < ROT_DIM, cos, 1.0)
    sup_ref[...] = jnp.where(d < half, -sin, 0.0)
    sdn_ref[...] = jnp.where((d >= half) & (d < ROT_DIM), sin, 0.0)


def _rope_tables(positions):
    T = positions.shape[0]
    out = jax.ShapeDtypeStruct((T, LANES), F32)
    return pl.pallas_call(
        _rope_table_kernel,
        grid=(1,),
        in_specs=[pl.BlockSpec((T, 1), lambda i: (0, 0))],
        out_specs=[pl.BlockSpec((T, LANES), lambda i: (0, 0))] * 3,
        out_shape=[out, out, out],
        name="rope_tables",
    )(positions.astype(F32).reshape(T, 1))


def _rope(x, cs, sup, sdn):
    w = x.shape[1]
    rep = w // LANES
    tile = lambda a: a if rep == 1 else jnp.concatenate([a] * rep, axis=1)
    half = ROT_DIM // 2
    return x * tile(cs) + pltpu.roll(x, w - half, 1) * tile(sup) + pltpu.roll(x, half, 1) * tile(sdn)


def _swa_kernel(sink_ref, q_ref, kc_ref, kp_ref, vc_ref, vp_ref,
                csc_ref, supc_ref, sdnc_ref, csp_ref, supp_ref, sdnp_ref, o_ref, *, n_kv):
    n = pl.program_id(1)
    W = SWA_WINDOW
    G = SWA_GROUP
    lane = lax.broadcasted_iota(jnp.int32, (1, LANES), 1)
    lo_half = lane < SWA_HEAD_DIM

    q = _rope(q_ref[...], csc_ref[...], supc_ref[...], sdnc_ref[...]) * (SWA_HEAD_DIM ** -0.5 * LOG2E)
    k = jnp.concatenate([_rope(kp_ref[...], csp_ref[...], supp_ref[...], sdnp_ref[...]),
                         _rope(kc_ref[...], csc_ref[...], supc_ref[...], sdnc_ref[...])], axis=0)
    v = jnp.concatenate([vp_ref[...], vc_ref[...]], axis=0)

    S = (G // 2) * W
    kj = lax.broadcasted_iota(jnp.int32, (2 * W, S), 0)
    qi = lax.broadcasted_iota(jnp.int32, (2 * W, S), 1) & (W - 1)
    diff = qi + W - kj
    allowed = (diff >= 0) & (diff < W) & ((kj >= W) | (n > 0))
    mask_bias = jnp.where(allowed, 0.0, -jnp.inf)
    half_mask = [lo_half, jnp.logical_not(lo_half)]
    hd = SWA_HEAD_DIM

    for jp in range(n_kv // 2):
        pair = slice(jp * LANES, (jp + 1) * LANES)
        kpair = k[:, pair]
        k_same = kpair.astype(BF16)
        k_swap = pltpu.roll(kpair, hd, 1).astype(BF16)
        vt = v[:, pair].T.astype(BF16)
        for jj in range(2):
            j = 2 * jp + jj
            outs_t = []
            for hh in range(2):
                heads = [j * G + g for g in range(hh, G, 2)]
                qs = jnp.concatenate(
                    [jnp.where(half_mask[hh], q[:, (h // 2) * LANES:(h // 2 + 1) * LANES], 0.0) for h in heads],
                    axis=0).astype(BF16)
                st = _dot_nt(k_same if hh == jj else k_swap, qs) + mask_bias
                sink = jnp.concatenate([jnp.full((1, W), sink_ref[h] * LOG2E, F32) for h in heads], axis=1)
                m = jnp.maximum(jnp.max(st, axis=0, keepdims=True), sink)
                e = jnp.exp2(st - m)
                denom = jnp.sum(e, axis=0, keepdims=True) + jnp.exp2(sink - m)
                outs_t.append(_dot(vt[jj * hd:(jj + 1) * hd, :], e.astype(BF16)) / denom)
            for gp in range(G // 2):
                pr = j * (G // 2) + gp
                cols = slice(gp * W, (gp + 1) * W)
                ot = jnp.concatenate([outs_t[0][:, cols], outs_t[1][:, cols]], axis=0)
                o_ref[:, pr * LANES:(pr + 1) * LANES] = ot.T.astype(o_ref.dtype)


def _swa(proj, B, T, sinks, tables, *, n_q, n_kv):
    M = proj.shape[0]
    W = SWA_WINDOW
    assert T % W == 0 and n_q == n_kv * SWA_GROUP and n_kv % 2 == 0
    nblk = T // W
    qw, kw = n_q * SWA_HEAD_DIM, n_kv * SWA_HEAD_DIM
    assert qw % kw == 0
    kcol = qw // kw
    cs, sup, sdn = tables
    cur = lambda b, n: (b * nblk + n, 0)
    tcur = lambda b, n: (n, 0)
    tprev = lambda b, n: (jnp.maximum(n - 1, 0), 0)
    tspec = lambda im: pl.BlockSpec((W, LANES), im)
    return pl.pallas_call(
        functools.partial(_swa_kernel, n_kv=n_kv),
        grid=(B, nblk),
        in_specs=[pl.BlockSpec(memory_space=pltpu.SMEM),
                  pl.BlockSpec((W, qw), cur),
                  pl.BlockSpec((W, kw), lambda b, n: (b * nblk + n, kcol)),
                  pl.BlockSpec((W, kw), lambda b, n: (b * nblk + jnp.maximum(n - 1, 0), kcol)),
                  pl.BlockSpec((W, kw), lambda b, n: (b * nblk + n, kcol + 1)),
                  pl.BlockSpec((W, kw), lambda b, n: (b * nblk + jnp.maximum(n - 1, 0), kcol + 1)),
                  tspec(tcur), tspec(tcur), tspec(tcur), tspec(tprev), tspec(tprev), tspec(tprev)],
        out_specs=pl.BlockSpec((W, qw), cur),
        out_shape=jax.ShapeDtypeStruct((M, qw), BF16),
        compiler_params=pltpu.CompilerParams(dimension_semantics=("parallel", "parallel")),
        name="swa",
    )(sinks, proj, proj, proj, proj, proj, cs, sup, sdn, cs, sup, sdn)


def _sconv_kernel(b_ref, c_ref, x_ref, w_ref, o_ref, carry_ref):
    tt = b_ref.shape[0]

    @pl.when(pl.program_id(1) == 0)
    def _():
        carry_ref[...] = jnp.zeros(carry_ref.shape, F32)

    u = c_ref[...] * x_ref[...]
    y = _causal_conv3(u, carry_ref[...], w_ref[...])
    carry_ref[...] = u[tt - SUBLANES:]
    o_ref[...] = (b_ref[...] * y).astype(o_ref.dtype)


def _sconv(proj, B, T, conv_w, *, tt):
    M, D3 = proj.shape
    D = D3 // 3
    assert T % tt == 0
    nt = T // tt
    blk = lambda off: pl.BlockSpec((tt, D), lambda b, t, off=off: (b * nt + t, off))
    return pl.pallas_call(
        _sconv_kernel,
        grid=(B, nt),
        in_specs=[blk(0), blk(1), blk(2), pl.BlockSpec(conv_w.shape, lambda b, t: (0, 0))],
        out_specs=pl.BlockSpec((tt, D), lambda b, t: (b * nt + t, 0)),
        out_shape=jax.ShapeDtypeStruct((M, D), BF16),
        scratch_shapes=[pltpu.VMEM((SUBLANES, D), F32)],
        compiler_params=pltpu.CompilerParams(dimension_semantics=("parallel", "arbitrary"),
                                             vmem_limit_bytes=_vmem_limit(12 * _nbytes((tt, D), F32), 4 * 2**20)),
        name="sconv",
    )(proj, proj, proj, conv_w)


def _fox_cum_kernel(f_ref, bf_ref, tri_ref, c_ref, carry_ref):
    tb = f_ref.shape[0]

    @pl.when(pl.program_id(1) == 0)
    def _():
        carry_ref[...] = jnp.zeros(carry_ref.shape, F32)

    x = f_ref[...] + bf_ref[...]
    log_f = jnp.minimum(x, 0.0) - jnp.log(1.0 + jnp.exp(-jnp.abs(x)))
    c = _dot_01_f32(tri_ref[...], log_f) + carry_ref[...]
    carry_ref[...] = c[tb - 1:tb]
    c_ref[...] = c * LOG2E


def _fox_cum(f_logit, B, T, b_f, *, tb):
    assert T % tb == 0
    nt = T // tb
    tri = jnp.asarray(np.tril(np.ones((tb, tb), np.float32)), BF16)
    return pl.pallas_call(
        _fox_cum_kernel,
        grid=(B, nt),
        in_specs=[pl.BlockSpec((tb, LANES), lambda b, t: (b * nt + t, 0)),
                  pl.BlockSpec((1, LANES), lambda b, t: (0, 0)),
                  pl.BlockSpec((tb, tb), lambda b, t: (0, 0))],
        out_specs=pl.BlockSpec((tb, LANES), lambda b, t: (b * nt + t, 0)),
        out_shape=jax.ShapeDtypeStruct((B * T, LANES), F32),
        scratch_shapes=[pltpu.VMEM((1, LANES), F32)],
        compiler_params=pltpu.CompilerParams(dimension_semantics=("parallel", "arbitrary")),
        name="fox_cum",
    )(f_logit, b_f.reshape(1, LANES), tri)


def _fox_kernel(q_ref, k_ref, v_ref, g_ref, c_ref, o_ref, kx_ref, vt_ref, acc_ref, m_ref, l_ref, s_ref, mb_ref,
                *, tq, tk):
    h = pl.program_id(1)
    qi = pl.program_id(2)
    nk = vt_ref.shape[0]
    hd = FOX_HEAD_DIM

    @pl.when(qi == 0)
    def _():
        r = lax.broadcasted_iota(jnp.int32, (LANES, LANES), 0)
        cc = lax.broadcasted_iota(jnp.int32, (LANES, LANES), 1)

        def sel(piece):
            hit = ((cc == piece) & (r == 2 * h)) | ((cc == piece + 3) & (r == 2 * h + 1))
            return jnp.where(hit, 1.0, 0.0).astype(BF16)

        s_hi, s_mid, s_lo = sel(0), sel(1), sel(2)
        for j in range(nk):
            rows = slice(j * tk, (j + 1) * tk)
            kx_ref[rows, :LANES] = k_ref[rows, :].astype(BF16)
            c = c_ref[rows, :]
            hi = c.astype(BF16)
            r1 = c - hi.astype(F32)
            mid = r1.astype(BF16)
            lo = (r1 - mid.astype(F32)).astype(BF16)
            kx_ref[rows, LANES:] = (_dot(hi, s_hi) + _dot(mid, s_mid) + _dot(lo, s_lo)).astype(BF16)
            vt_ref[j] = v_ref[rows, :].T.astype(BF16)

    lane = lax.broadcasted_iota(jnp.int32, (1, LANES), 1)
    lo_half = lane < hd
    q2 = q_ref[...] * (hd ** -0.5 * LOG2E)
    qx = []
    for hh in range(2):
        qm = jnp.where(lo_half if hh == 0 else jnp.logical_not(lo_half), q2, 0.0)
        minus_one = jnp.where((lane >= 3 * hh) & (lane < 3 * hh + 3), -1.0, 0.0)
        qx.append(jnp.concatenate([qm, jnp.broadcast_to(minus_one, (tq, LANES))], axis=1).astype(BF16))

    m_ref[...] = jnp.full(m_ref.shape, -jnp.inf, F32)
    l_ref[...] = jnp.zeros(l_ref.shape, F32)
    acc_ref[...] = jnp.zeros(acc_ref.shape, F32)

    def produce(j, buf, diag, q0=0):
        kxb = kx_ref[pl.ds(pl.multiple_of(j * tk, tk), tk), :]
        for hh in range(2):
            st = _dot_nt(kxb, qx[hh][q0:, :])
            if diag is not None:
                key_i = lax.broadcasted_iota(jnp.int32, st.shape, 0) + diag * tk
                qry_i = lax.broadcasted_iota(jnp.int32, st.shape, 1) + q0
                st = jnp.where(key_i <= qry_i, st, -jnp.inf)
            s_ref[buf, hh, :, q0:] = st
            mb_ref[buf, hh:hh + 1, q0:] = jnp.max(st, axis=0, keepdims=True)

    def consume(j, buf, q0=0):
        vtb = vt_ref[j]
        for hh in range(2):
            m_old = m_ref[hh:hh + 1, q0:]
            m_new = jnp.maximum(m_old, mb_ref[buf, hh:hh + 1, q0:])
            alpha = jnp.exp2(m_old - m_new)
            p = jnp.exp2(s_ref[buf, hh, :, q0:] - m_new)
            l_ref[hh:hh + 1, q0:] = alpha * l_ref[hh:hh + 1, q0:] + jnp.sum(p, axis=0, keepdims=True)
            rows = slice(hh * hd, (hh + 1) * hd)
            acc_ref[rows, q0:] = alpha * acc_ref[rows, q0:] + _dot(vtb[rows, :], p.astype(BF16))
            m_ref[hh:hh + 1, q0:] = m_new

    assert tq == 2 * tk

    def pair(i, next_is_diagonal):
        produce(2 * i + 1, 1, None)
        consume(2 * i, 0)
        produce(2 * i + 2, 0, 0 if next_is_diagonal else None)
        consume(2 * i + 1, 1)

    def full_pair(i, carry):
        pair(i, False)
        return carry

    def finish():
        produce(2 * qi + 1, 1, 1, q0=tk)
        consume(2 * qi, 0)
        consume(2 * qi + 1, 1, q0=tk)
        ot = jnp.concatenate([acc_ref[:hd, :] / l_ref[0:1, :], acc_ref[hd:, :] / l_ref[1:2, :]], axis=0)
        o_ref[...] = (ot.T * _sigmoid(g_ref[...])).astype(o_ref.dtype)

    @pl.when(qi == 0)
    def _():
        produce(0, 0, 0)
        finish()

    @pl.when(qi > 0)
    def _():
        produce(0, 0, None)
        lax.fori_loop(0, qi - 1, full_pair, 0)
        pair(qi - 1, True)
        finish()


def _fox_attn(qkvg, B, T, c2, *, n_heads, tq, tk):
    M, W4 = qkvg.shape
    W = W4 // 4
    assert W == n_heads * FOX_HEAD_DIM and T % tq == 0 and tq % tk == 0 and n_heads % 2 == 0
    hp = n_heads // 2
    nq, nk = T // tq, T // tk
    limit = _vmem_limit(6 * _nbytes((T, LANES), F32), _nbytes((T, 2 * LANES), BF16), _nbytes((T, LANES), BF16),
                        16 * _nbytes((tk, tq), F32), 8 * 2**20)
    return pl.pallas_call(
        functools.partial(_fox_kernel, tq=tq, tk=tk),
        grid=(B, hp, nq),
        in_specs=[pl.BlockSpec((tq, LANES), lambda b, h, i: (b * nq + i, h)),
                  pl.BlockSpec((T, LANES), lambda b, h, i: (b, hp + h)),
                  pl.BlockSpec((T, LANES), lambda b, h, i: (b, 2 * hp + h)),
                  pl.BlockSpec((tq, LANES), lambda b, h, i: (b * nq + i, 3 * hp + h)),
                  pl.BlockSpec((T, LANES), lambda b, h, i: (b, 0))],
        out_specs=pl.BlockSpec((tq, LANES), lambda b, h, i: (b * nq + i, h)),
        out_shape=jax.ShapeDtypeStruct((M, W), BF16),
        scratch_shapes=[pltpu.VMEM((T, 2 * LANES), BF16), pltpu.VMEM((nk, LANES, tk), BF16),
                        pltpu.VMEM((LANES, tq), F32), pltpu.VMEM((2, tq), F32), pltpu.VMEM((2, tq), F32),
                        pltpu.VMEM((2, 2, tk, tq), F32), pltpu.VMEM((2, 2, tq), F32)],
        compiler_params=pltpu.CompilerParams(dimension_semantics=("parallel", "parallel", "arbitrary"),
                                             vmem_limit_bytes=limit),
        name="fox_attn",
    )(qkvg, qkvg, qkvg, qkvg, c2)


def _pick(n, prefs):
    for t in prefs:
        if n % t == 0:
            return t
    raise ValueError(f"no tile in {prefs} divides {n}")


def kernel(x, positions, mix_pre_g, mix_post_g, ffn_pre_g, ffn_post_g, hgrn_w_in, hgrn_w_out, hgrn_norm_g, hgrn_lb_param, swa_w_in, swa_w_out, swa_sinks, sc_w_in, sc_conv_w, sc_w_out, fox_w_in, fox_b_f, fox_w_out, ffn_w_up, ffn_conv_w, ffn_conv_b, ffn_w_down):
    B, T, D = x.shape
    M = B * T
    depth = mix_pre_g.shape[0]
    xf = x.reshape(M, D)
    tm_proj = _pick(M, (1024, 512, 256, 128))
    tm_out = _pick(M, (512, 256, 128))
    tm_ffn = _pick(T, (512, 256, 128))
    bf = lambda w: w.astype(BF16)
    tables = None

    for i in range(depth):
        m, j = i % N_MIXERS, i // N_MIXERS
        if m == 0:
            proj = _norm_matmul(xf, mix_pre_g[i], bf(hgrn_w_in[j]), tm=tm_proj, tn=None)
            a = _hgrn(proj, B, T, hgrn_lb_param, hgrn_norm_g[j], layer=i, tc=_pick(T, (512, 256, 128, 64)),
                      heads_per_step=8)
            w_out = hgrn_w_out[j]
        elif m == 1:
            n_q = swa_sinks.shape[1]
            n_kv = n_q // SWA_GROUP
            if tables is None:
                tables = _rope_tables(positions)
            proj = _norm_matmul(xf, mix_pre_g[i], bf(swa_w_in[j]), tm=tm_proj, tn=None)
            a = _swa(proj, B, T, swa_sinks[j], tables, n_q=n_q, n_kv=n_kv)
            w_out = swa_w_out[j]
        elif m == 2:
            proj = _norm_matmul(xf, mix_pre_g[i], bf(sc_w_in[j]), tm=tm_proj, tn=None)
            a = _sconv(proj, B, T, sc_conv_w[j], tt=_pick(T, (256, 128)))
            w_out = sc_w_out[j]
        else:
            n_heads = fox_b_f.shape[1]
            W = n_heads * FOX_HEAD_DIM
            w_in = fox_w_in[j]
            w_qkvg = bf(jnp.concatenate([w_in[:, :3 * W], w_in[:, 3 * W + n_heads:]], axis=1))
            w_f = bf(jnp.pad(w_in[:, 3 * W:3 * W + n_heads], ((0, 0), (0, LANES - n_heads))))
            b_f = jnp.pad(fox_b_f[j], (0, LANES - n_heads))
            qkvg = _norm_matmul(xf, mix_pre_g[i], w_qkvg, tm=tm_proj, tn=None)
            f_logit = _norm_matmul(xf, mix_pre_g[i], w_f, tm=tm_proj, tn=LANES)
            c2 = _fox_cum(f_logit, B, T, b_f, tb=_pick(T, (256, 128)))
            a = _fox_attn(qkvg, B, T, c2, n_heads=n_heads, tq=1024, tk=512)
            w_out = fox_w_out[j]
        xf = _out_proj(a, bf(w_out), mix_post_g[i], xf, tm=tm_out)
        xf = _ffn(xf, B, T, ffn_pre_g[i], bf(ffn_w_up[i]), ffn_conv_w[i], ffn_conv_b[i], bf(ffn_w_down[i]),
                  ffn_post_g[i], tm=tm_ffn, tf=_pick(ffn_w_down.shape[1], (512, 256, 128)))
    return xf.reshape(B, T, D)
```
